```python
import jax, jax.numpy as jnp
from jax import lax
import numpy as np

D_MODEL = 1024
BATCH = 8
SEQ = 4096
DEPTH = 1

DN_HEADS = 8
DN_DK = 128
DN_DV = 128
DN_CONV = 4
DN_CHUNK = 64
DIL_GROUPS = ((128, 1), (512, 4), (2048, 16))
DIL_HEADS = 4
DIL_DH = 128
ATT_BLOCK = 128
NORM_EPS = 1e-6

N_DIL = len(DIL_GROUPS)
DN_QK_W = DN_HEADS * DN_DK
DN_V_W = DN_HEADS * DN_DV
DIL_W = DIL_HEADS * DIL_DH
PROJ_SIZES = (DN_QK_W, DN_QK_W, DN_V_W, DN_V_W, DN_HEADS, DN_HEADS,
              N_DIL * DIL_W, N_DIL * DIL_W, N_DIL * DIL_W, DIL_W, D_MODEL, D_MODEL)
PROJ_W = sum(PROJ_SIZES)

kernel_name = "hybrid_deltanet_dilated_alibi_block"


def _rmsnorm(x, w):
    xf = x.astype(jnp.float32)
    y = xf * lax.rsqrt(jnp.mean(xf * xf, axis=-1, keepdims=True) + NORM_EPS)
    return (y * w.astype(jnp.float32)).astype(x.dtype)


def _l2norm(x):
    return x * lax.rsqrt(jnp.sum(x * x, axis=-1, keepdims=True) + NORM_EPS)


def _split_cols(t, sizes):
    out, start = [], 0
    for s in sizes:
        out.append(t[..., start:start + s])
        start += s
    return out


def _causal_conv(u, w):
    K, C = w.shape
    return lax.conv_general_dilated(
        u, w[:, None, :].astype(u.dtype), window_strides=(1,), padding=[(K - 1, 0)],
        dimension_numbers=('NWC', 'WIO', 'NWC'), feature_group_count=C)


def _alibi_slopes(n):
    return 2.0 ** (-8.0 * jnp.arange(1, n + 1, dtype=jnp.float32) / n)


def _gated_delta_rule(q, k, v, beta, g):
    Bn, Sn, H, dk = q.shape
    dv = v.shape[-1]
    C = DN_CHUNK
    N = Sn // C

    def chunk(t):
        t = t.reshape((Bn, N, C, H) + t.shape[3:])
        return jnp.moveaxis(t, 3, 1)

    q = chunk(q) * (dk ** -0.5)
    k, v, beta, g = chunk(k), chunk(v), chunk(beta), chunk(g)
    gc = jnp.cumsum(g, axis=-1)
    causal = jnp.tril(jnp.ones((C, C), dtype=bool))
    strict = jnp.tril(jnp.ones((C, C), dtype=bool), -1)
    gamma = jnp.exp(jnp.where(causal, gc[..., :, None] - gc[..., None, :], -jnp.inf))

    kb = k * beta[..., None]
    a = jnp.einsum('bhnid,bhnjd->bhnij', kb, k) * gamma
    m = jnp.where(strict, a, 0.0) + jnp.eye(C, dtype=a.dtype)
    rhs = jnp.concatenate([v * beta[..., None], kb * jnp.exp(gc)[..., None]], axis=-1)
    sol = lax.linalg.triangular_solve(m, rhs, left_side=True, lower=True, unit_diagonal=True)
    u, w = sol[..., :dv], sol[..., dv:]

    aqk = jnp.einsum('bhnid,bhnjd->bhnij', q, k) * gamma
    qd = q * jnp.exp(gc)[..., None]
    kd = k * jnp.exp(gc[..., -1:] - gc)[..., None]
    dlast = jnp.exp(gc[..., -1])

    def step(state, xs):
        u_n, w_n, aqk_n, qd_n, kd_n, dl_n = xs
        v_new = u_n - jnp.einsum('bhck,bhkv->bhcv', w_n, state)
        o = jnp.einsum('bhck,bhkv->bhcv', qd_n, state) + jnp.einsum('bhij,bhjv->bhiv', aqk_n, v_new)
        state = state * dl_n[..., None, None] + jnp.einsum('bhck,bhcv->bhkv', kd_n, v_new)
        return state, o

    xs = (jnp.moveaxis(u, 2, 0), jnp.moveaxis(w, 2, 0), jnp.moveaxis(aqk, 2, 0),
          jnp.moveaxis(qd, 2, 0), jnp.moveaxis(kd, 2, 0), jnp.moveaxis(dlast, 2, 0))
    s0 = jnp.zeros((Bn, H, dk, dv), jnp.float32)
    _, o = lax.scan(step, s0, xs)
    o = jnp.moveaxis(o, 0, 2)
    return jnp.moveaxis(o, 1, 3).reshape(Bn, Sn, H, dv)


def _dilated_group(q, k, v, window, dilation, slopes):
    Bn, Sn, H, dh = q.shape
    L = Sn // dilation
    span = window // dilation
    nb = -(-L // ATT_BLOCK)
    n_prev = -(-span // ATT_BLOCK)
    Lp = nb * ATT_BLOCK
    KW = (n_prev + 1) * ATT_BLOCK

    def sub(t):
        return jnp.swapaxes(t.reshape(Bn, L, dilation, H, dh), 1, 2)

    qb = jnp.pad(sub(q), ((0, 0), (0, 0), (0, Lp - L), (0, 0), (0, 0)))
    qb = qb.reshape(Bn, dilation, nb, ATT_BLOCK, H, dh)

    def windows(t):
        t = jnp.pad(sub(t), ((0, 0), (0, 0), (n_prev * ATT_BLOCK, Lp - L), (0, 0), (0, 0)))
        t = t.reshape(Bn, dilation, nb + n_prev, ATT_BLOCK, H, dh)
        return jnp.concatenate([t[:, :, j:j + nb] for j in range(n_prev + 1)], axis=3)

    kw, vw = windows(k), windows(v)
    lq = (jnp.arange(nb)[:, None, None] * ATT_BLOCK + jnp.arange(ATT_BLOCK)[None, :, None])
    dist = n_prev * ATT_BLOCK + jnp.arange(ATT_BLOCK)[:, None] - jnp.arange(KW)[None, :]
    valid = (dist >= 0) & (dist <= span) & (lq - dist >= 0)
    alibi = slopes[:, None, None] * (dist * dilation).astype(jnp.float32)[None]

    s = jnp.einsum('bdnqhe,bdnkhe->bdnhqk', qb, kw).astype(jnp.float32) * (dh ** -0.5) - alibi
    s = jnp.where(valid[None, None, :, None], s, -jnp.inf)
    mx = jnp.max(s, axis=-1)
    p = jnp.exp(s - mx[..., None])
    den = jnp.sum(p, axis=-1)
    num = jnp.einsum('bdnhqk,bdnkhe->bdnqhe', p, vw.astype(jnp.float32))

    def back(t):
        t = t.reshape((Bn, dilation, Lp) + t.shape[4:])[:, :, :L]
        return jnp.swapaxes(t, 1, 2).reshape((Bn, Sn) + t.shape[3:])

    return back(num), back(jnp.swapaxes(den, 3, 4)), back(jnp.swapaxes(mx, 3, 4))


def _dilated_attention(q, k, v):
    Bn, Sn, _ = q.shape
    q = q.reshape(Bn, Sn, N_DIL, DIL_HEADS, DIL_DH)
    k = k.reshape(Bn, Sn, N_DIL, DIL_HEADS, DIL_DH)
    v = v.reshape(Bn, Sn, N_DIL, DIL_HEADS, DIL_DH)
    slopes = _alibi_slopes(N_DIL * DIL_HEADS).reshape(N_DIL, DIL_HEADS)
    parts = [_dilated_group(q[:, :, i], k[:, :, i], v[:, :, i], win, dil, slopes[i])
             for i, (win, dil) in enumerate(DIL_GROUPS)]
    m_all = parts[0][2]
    for _, _, mx in parts[1:]:
        m_all = jnp.maximum(m_all, mx)
    num = 0.0
    den = 0.0
    for nm, dn, mx in parts:
        sc = jnp.exp(mx - m_all)
        num = num + nm * sc[..., None]
        den = den + dn * sc
    return (num / den[..., None]).reshape(Bn, Sn, DIL_W)


def setup_inputs(seed: int = 0) -> dict:
    key = jax.random.key(seed)
    ks = jax.random.split(key, 12)
    f32 = jnp.float32
    x = jax.random.normal(ks[0], (BATCH, SEQ, D_MODEL), f32)
    norm_w = 1.0 + 0.01 * jax.random.normal(ks[1], (DEPTH, D_MODEL), f32)
    w_in = jax.random.normal(ks[2], (DEPTH, D_MODEL, PROJ_W), f32) * D_MODEL ** -0.5
    conv_w = jax.random.normal(ks[3], (DEPTH, DN_CONV, 2 * DN_QK_W + DN_V_W), f32) * DN_CONV ** -0.5
    a_log = jnp.log(jax.random.uniform(ks[4], (DEPTH, DN_HEADS), f32, 1.0, 16.0))
    dt = jnp.exp(jax.random.uniform(ks[5], (DEPTH, DN_HEADS), f32, np.log(1e-3), np.log(1e-1)))
    dt_bias = dt + jnp.log(-jnp.expm1(-dt))
    dn_norm_w = 1.0 + 0.01 * jax.random.normal(ks[6], (DEPTH, DN_DV), f32)
    w_o_dn = jax.random.normal(ks[7], (DEPTH, DN_V_W, D_MODEL), f32) * DN_V_W ** -0.5
    w_o_dil = jax.random.normal(ks[8], (DEPTH, DIL_W, D_MODEL), f32) * DIL_W ** -0.5
    w_out = jax.random.normal(ks[9], (DEPTH, D_MODEL, D_MODEL), f32) * D_MODEL ** -0.5
    final_norm_w = 1.0 + 0.01 * jax.random.normal(ks[10], (D_MODEL,), f32)
    return {"x": x, "norm_w": norm_w, "w_in": w_in, "conv_w": conv_w, "a_log": a_log,
            "dt_bias": dt_bias, "dn_norm_w": dn_norm_w, "w_o_dn": w_o_dn, "w_o_dil": w_o_dil,
            "w_out": w_out, "final_norm_w": final_norm_w}


def reference(x, norm_w, w_in, conv_w, a_log, dt_bias, dn_norm_w, w_o_dn, w_o_dil, w_out, final_norm_w):
    Bn, Sn, _ = x.shape
    f32 = jnp.float32
    for l in range(DEPTH):
        h = _rmsnorm(x, norm_w[l])
        proj = h @ w_in[l]
        (q_a, k_a, v_a, z_a, b_a, a_a, q_b, k_b, v_b, z_b, g_a, g_b) = _split_cols(proj, PROJ_SIZES)

        qkv = jax.nn.silu(_causal_conv(jnp.concatenate([q_a, k_a, v_a], axis=-1), conv_w[l]))
        q_a, k_a, v_a = _split_cols(qkv, (DN_QK_W, DN_QK_W, DN_V_W))
        qh = _l2norm(q_a.reshape(Bn, Sn, DN_HEADS, DN_DK).astype(f32))
        kh = _l2norm(k_a.reshape(Bn, Sn, DN_HEADS, DN_DK).astype(f32))
        vh = v_a.reshape(Bn, Sn, DN_HEADS, DN_DV).astype(f32)
        beta = jax.nn.sigmoid(b_a.astype(f32))
        g = -jnp.exp(a_log[l].astype(f32)) * jax.nn.softplus(a_a.astype(f32) + dt_bias[l].astype(f32))
        o_a = _gated_delta_rule(qh, kh, vh, beta, g)
        o_a = _rmsnorm(o_a, dn_norm_w[l]) * jax.nn.silu(z_a.reshape(Bn, Sn, DN_HEADS, DN_DV).astype(f32))
        y_a = o_a.reshape(Bn, Sn, DN_V_W).astype(x.dtype) @ w_o_dn[l]

        o_b = _dilated_attention(q_b, k_b, v_b) * jax.nn.silu(z_b.astype(f32))
        y_b = o_b.astype(x.dtype) @ w_o_dil[l]

        merged = jax.nn.sigmoid(g_a) * y_a + jax.nn.sigmoid(g_b) * y_b
        x = x + merged @ w_out[l]
    return _rmsnorm(x, final_norm_w)
```

```python
import functools
import math

import jax
import jax.numpy as jnp
import numpy as np
from jax import lax
from jax.experimental import pallas as pl
from jax.experimental.pallas import tpu as pltpu

F32 = jnp.float32
BF16 = jnp.bfloat16

D_MODEL = 1024
DN_HEADS = 8
DN_DK = 128
DN_DV = 128
DN_CONV = 4
DN_CHUNK = 64
DIL_GROUPS = ((128, 1), (512, 4), (2048, 16))
DIL_HEADS = 4
DIL_DH = 128
ATT_BLOCK = 128
NORM_EPS = 1e-6
N_DIL = len(DIL_GROUPS)
DN_W = DN_HEADS * DN_DK
DIL_W = DIL_HEADS * DIL_DH
GROUP_W = 3 * DIL_W
PROJ_SIZES = (DN_W, DN_W, DN_W, DN_W, DN_HEADS, DN_HEADS,
              N_DIL * DIL_W, N_DIL * DIL_W, N_DIL * DIL_W, DIL_W, D_MODEL, D_MODEL)

LANES = 128
SMALL_PAD = 512
VMEM_LIMIT = 56 * 1024 * 1024

NORM_TM = 512
PROJ_TM = 2048
PROJ_TN = 512
DN_TB = 64
OUT_TM = 1024


def _sigmoid(x):
    return 1.0 / (1.0 + jnp.exp(-x))


def _silu(x):
    return x * _sigmoid(x)


def _softplus(x):
    return jnp.maximum(x, 0.0) + jnp.log1p(jnp.exp(-jnp.abs(x)))


def _mm(a, b):
    return lax.dot_general(a.astype(BF16), b.astype(BF16), (((1,), (0,)), ((), ())),
                           preferred_element_type=F32)


def _mm_nt(a, b):
    return lax.dot_general(a.astype(BF16), b.astype(BF16), (((1,), (1,)), ((), ())),
                           preferred_element_type=F32)


def _mm_tn(a, b):
    return lax.dot_general(a.astype(BF16), b.astype(BF16), (((0,), (0,)), ((), ())),
                           preferred_element_type=F32)


def _mm_f32(a, b):
    return lax.dot_general(a, b, (((1,), (0,)), ((), ())), precision=lax.Precision.HIGHEST,
                           preferred_element_type=F32)


def _norm_kernel(x_ref, w_ref, h_ref, h4_ref, h16_ref, slab_ref):
    x = x_ref[...]
    y = x * lax.rsqrt(jnp.mean(x * x, axis=-1, keepdims=True) + NORM_EPS) * w_ref[...]
    h_ref[...] = y.astype(BF16)
    n_slabs = D_MODEL // LANES
    for c in range(n_slabs):
        slab_ref[c] = y[:, c * LANES:(c + 1) * LANES]
    tm = x.shape[0]
    for d, out in ((4, h4_ref), (16, h16_ref)):
        n = tm // d
        for r in range(d):
            for c in range(n_slabs):
                out[r, :, c * LANES:(c + 1) * LANES] = slab_ref[c, pl.ds(r, n, stride=d), :].astype(BF16)


def _norm_call(x, w):
    B, S, D = x.shape
    tm = NORM_TM
    nt = S // tm
    return pl.pallas_call(
        _norm_kernel,
        grid=(B, nt),
        in_specs=[pl.BlockSpec((None, tm, D), lambda b, i: (b, i, 0)),
                  pl.BlockSpec((1, D), lambda b, i: (0, 0))],
        out_specs=[pl.BlockSpec((None, tm, D), lambda b, i: (b, i, 0)),
                   pl.BlockSpec((None, 4, tm // 4, D), lambda b, i: (b, 0, i, 0)),
                   pl.BlockSpec((None, 16, tm // 16, D), lambda b, i: (b, 0, i, 0))],
        out_shape=[jax.ShapeDtypeStruct((B, S, D), BF16),
                   jax.ShapeDtypeStruct((B, 4, S // 4, D), BF16),
                   jax.ShapeDtypeStruct((B, 16, S // 16, D), BF16)],
        scratch_shapes=[pltpu.VMEM((D // LANES, tm, LANES), F32)],
        compiler_params=pltpu.CompilerParams(dimension_semantics=("arbitrary", "arbitrary"),
                                             vmem_limit_bytes=VMEM_LIMIT),
        name="norm_deinterleave",
    )(x, w.reshape(1, D))


_ACTS = {"none": lambda v: v, "silu": _silu, "sigmoid": _sigmoid}


def _proj_kernel(h_ref, w_ref, o_ref, *, segs):
    acc = jnp.dot(h_ref[...], w_ref[...], preferred_element_type=F32)
    if len(segs) == 1:
        o_ref[...] = _ACTS[segs[0][2]](acc).astype(o_ref.dtype)
        return
    j = pl.program_id(1)
    for lo, hi, act in segs:
        @pl.when((j >= lo) & (j < hi))
        def _(act=act):
            o_ref[...] = _ACTS[act](acc).astype(o_ref.dtype)


def _proj_call(h2d, w, segs, out_dtype, name):
    T, D = h2d.shape
    N = w.shape[1]
    tm, tn = PROJ_TM, PROJ_TN
    return pl.pallas_call(
        functools.partial(_proj_kernel, segs=segs),
        grid=(T // tm, N // tn),
        in_specs=[pl.BlockSpec((tm, D), lambda i, j: (i, 0)),
                  pl.BlockSpec((D, tn), lambda i, j: (0, j))],
        out_specs=pl.BlockSpec((tm, tn), lambda i, j: (i, j)),
        out_shape=jax.ShapeDtypeStruct((T, N), out_dtype),
        compiler_params=pltpu.CompilerParams(dimension_semantics=("arbitrary", "arbitrary"),
                                             vmem_limit_bytes=VMEM_LIMIT),
        name=name,
    )(h2d, w)


def _deltanet_kernel(qkv_ref, z_ref, cw_ref, hp_ref, nw_ref, o_ref, xs_ref, st_ref):
    C = DN_CHUNK
    tb = o_ref.shape[0]
    qkv_w = 3 * DN_W
    i = pl.program_id(1)

    @pl.when(i == 0)
    def _():
        st_ref[...] = jnp.zeros_like(st_ref)
        xs_ref[0:8, :] = jnp.zeros((8, qkv_w), F32)

    @pl.when(i > 0)
    def _():
        xs_ref[0:8, :] = xs_ref[tb:tb + 8, :]

    xs_ref[8:tb + 8, :] = qkv_ref[:, 0:qkv_w]

    row = lax.broadcasted_iota(jnp.int32, (C, C), 0)
    col = lax.broadcasted_iota(jnp.int32, (C, C), 1)
    causal = row >= col
    strict = row > col
    ltri = causal.astype(F32)
    hp = hp_ref[...]
    neg_a = -jnp.exp(hp[0:1, :])
    dt_b = hp[1:2, :]
    nw = nw_ref[...]

    def conv_silu(base, c0):
        acc = cw_ref[0:1, c0:c0 + LANES] * xs_ref[pl.ds(base + 5, C), c0:c0 + LANES]
        for k in range(1, DN_CONV):
            acc = acc + cw_ref[k:k + 1, c0:c0 + LANES] * xs_ref[pl.ds(base + 5 + k, C), c0:c0 + LANES]
        return _silu(acc)

    def chunk_body(c):
        base = c * C
        sm = qkv_ref[pl.ds(base, C), qkv_w:qkv_w + LANES]
        beta_all = _sigmoid(sm)
        g_all = neg_a * _softplus(sm + dt_b)
        gc_all = _mm_f32(ltri, g_all)
        gc_t = jnp.concatenate([gc_all, jnp.zeros_like(gc_all)], axis=0).T

        for h in range(DN_HEADS):
            lane = DN_HEADS + h
            gcol = gc_all[:, lane:lane + 1]
            grow = gc_t[lane:lane + 1, 0:C]
            glast = gc_all[C - 1:C, lane:lane + 1]
            gamma = jnp.exp(jnp.where(causal, gcol - grow, -jnp.inf))
            eg = jnp.exp(gcol)
            ek = jnp.exp(glast - gcol)
            dl = jnp.exp(glast)
            beta = beta_all[:, h:h + 1]

            q = conv_silu(base, h * DN_DK)
            k = conv_silu(base, DN_W + h * DN_DK)
            v = conv_silu(base, 2 * DN_W + h * DN_DV)
            q = q * (lax.rsqrt(jnp.sum(q * q, axis=-1, keepdims=True) + NORM_EPS) * (DN_DK ** -0.5))
            k = k * lax.rsqrt(jnp.sum(k * k, axis=-1, keepdims=True) + NORM_EPS)

            kb = k * beta
            neg_m = jnp.where(strict, -(_mm_nt(kb, k) * gamma), 0.0)
            x = jnp.concatenate([v * beta, kb * eg], axis=1)
            x = x + _mm(neg_m, x)
            p = neg_m
            for _ in range(5):
                p = _mm(p, p)
                x = x + _mm(p, x)
            u = x[:, 0:DN_DV]
            w = x[:, DN_DV:]

            aqk = _mm_nt(q, k) * gamma
            qd = q * eg
            kd = k * ek
            s = st_ref[h]
            ws_qs = _mm(jnp.concatenate([w, qd], axis=0), s)
            v_new = u - ws_qs[0:C]
            o = ws_qs[C:] + _mm(aqk, v_new)
            st_ref[h] = s * dl + _mm_tn(kd, v_new)

            y = o * lax.rsqrt(jnp.mean(o * o, axis=-1, keepdims=True) + NORM_EPS) * nw
            gate = z_ref[pl.ds(base, C), h * DN_DV:(h + 1) * DN_DV].astype(F32)
            o_ref[pl.ds(base, C), h * DN_DV:(h + 1) * DN_DV] = (y * gate).astype(o_ref.dtype)

    for c in range(tb // C):
        chunk_body(c)


def _deltanet_call(p1, p2, conv_w, a_log, dt_bias, dn_norm_w, B, S):
    tb = DN_TB
    nt = S // tb
    w1 = p1.shape[1]
    cw = jnp.zeros((8, 3 * DN_W), F32).at[0:DN_CONV].set(conv_w.astype(F32))
    hp = jnp.zeros((8, LANES), F32)
    hp = hp.at[0, DN_HEADS:2 * DN_HEADS].set(a_log.astype(F32)).at[1, DN_HEADS:2 * DN_HEADS].set(dt_bias.astype(F32))
    nw = dn_norm_w.astype(F32).reshape(1, DN_DV)
    return pl.pallas_call(
        _deltanet_kernel,
        grid=(B, nt),
        in_specs=[pl.BlockSpec((tb, w1), lambda b, i: (b * nt + i, 0)),
                  pl.BlockSpec((tb, DN_W), lambda b, i: (b * nt + i, 0)),
                  pl.BlockSpec((8, 3 * DN_W), lambda b, i: (0, 0)),
                  pl.BlockSpec((8, LANES), lambda b, i: (0, 0)),
                  pl.BlockSpec((1, DN_DV), lambda b, i: (0, 0))],
        out_specs=pl.BlockSpec((tb, DN_W), lambda b, i: (b * nt + i, 0)),
        out_shape=jax.ShapeDtypeStruct((B * S, DN_W), BF16),
        scratch_shapes=[pltpu.VMEM((tb + 8, 3 * DN_W), F32),
                        pltpu.VMEM((DN_HEADS, DN_DK, DN_DV), F32)],
        compiler_params=pltpu.CompilerParams(dimension_semantics=("arbitrary", "arbitrary"),
                                             vmem_limit_bytes=VMEM_LIMIT),
        name="deltanet",
    )(p1, p2, cw, hp, nw)


def _alibi_slope(idx):
    n = N_DIL * DIL_HEADS
    return 2.0 ** (-8.0 * (idx + 1) / n)


def _attn_kernel(q0_ref, k0_ref, v0_ref, q1_ref, k1_ref, v1_ref, q2_ref, k2_ref, v2_ref, z_ref, o_ref,
                 num_ref, den_ref, mx_ref):
    Q = ATT_BLOCK
    hd = pl.program_id(1)
    span = DIL_GROUPS[0][0] // DIL_GROUPS[0][1]
    dist = (Q + lax.broadcasted_iota(jnp.int32, (Q, 2 * Q), 0) - lax.broadcasted_iota(jnp.int32, (Q, 2 * Q), 1))
    valid = (dist >= 0) & (dist <= span)
    distf = dist.astype(F32)
    scale = DIL_DH ** -0.5

    def slope_of(g):
        s = jnp.float32(_alibi_slope(g * DIL_HEADS))
        for h in range(1, DIL_HEADS):
            s = jnp.where(hd == h, jnp.float32(_alibi_slope(g * DIL_HEADS + h)), s)
        return s

    def block(q, k, v, bias):
        s = _mm_nt(q, k) * scale + bias
        mx = jnp.max(s, axis=-1, keepdims=True)
        p = jnp.exp(s - mx)
        den = jnp.sum(p, axis=-1, keepdims=True)
        num = _mm(p, v)
        return num, jnp.broadcast_to(den, (Q, LANES)), jnp.broadcast_to(mx, (Q, LANES))

    def merge(a, b):
        (n1, d1, m1), (n2, d2, m2) = a, b
        m = jnp.maximum(m1, m2)
        a1 = jnp.exp(m1 - m)
        a2 = jnp.exp(m2 - m)
        return n1 * a1 + n2 * a2, d1 * a1 + d2 * a2, m

    def run_group(g, q_ref, k_ref, v_ref, emit):
        d = DIL_GROUPS[g][1]
        bias2 = jnp.where(valid, -(slope_of(g) * float(d)) * distf, -jnp.inf)
        bias1 = bias2[:, Q:]
        nb = q_ref.shape[-2] // Q

        def sub(r, carry):
            if d == 1:
                qr, kr, vr = q_ref, k_ref, v_ref
            else:
                qr, kr, vr = q_ref.at[r], k_ref.at[r], v_ref.at[r]
            emit(r, 0, block(qr[0:Q, :], kr[0:Q, :], vr[0:Q, :], bias1))

            def qblock(j, c2):
                lo = pl.multiple_of((j - 1) * Q, Q)
                cur = pl.multiple_of(j * Q, Q)
                emit(r, j, block(qr[pl.ds(cur, Q), :], kr[pl.ds(lo, 2 * Q), :], vr[pl.ds(lo, 2 * Q), :], bias2))
                return c2

            lax.fori_loop(1, nb, qblock, 0)
            return carry

        if d == 1:
            sub(0, 0)
        else:
            lax.fori_loop(0, d, sub, 0)

    def rows(g, r, j):
        d = DIL_GROUPS[g][1]
        if d == 1:
            return pl.ds(j * Q, Q) if isinstance(j, int) else pl.ds(pl.multiple_of(j * Q, Q), Q)
        return pl.ds(j * (Q * d) + r, Q, stride=d)

    def emit_first(r, j, res):
        idx = rows(2, r, j)
        num_ref[idx, :] = res[0]
        den_ref[idx, :] = res[1]
        mx_ref[idx, :] = res[2]

    def emit_second(r, j, res):
        idx = rows(1, r, j)
        n, dn, m = merge((num_ref[idx, :], den_ref[idx, :], mx_ref[idx, :]), res)
        num_ref[idx, :] = n
        den_ref[idx, :] = dn
        mx_ref[idx, :] = m

    def emit_last(r, j, res):
        idx = rows(0, r, j)
        n, dn, _ = merge((num_ref[idx, :], den_ref[idx, :], mx_ref[idx, :]), res)
        o_ref[idx, :] = ((n / dn) * z_ref[idx, :].astype(F32)).astype(o_ref.dtype)

    run_group(2, q2_ref, k2_ref, v2_ref, emit_first)
    run_group(1, q1_ref, k1_ref, v1_ref, emit_second)
    run_group(0, q0_ref, k0_ref, v0_ref, emit_last)


def _attn_call(p2, off0, zb_off, g1, g2, B, S):
    H = DIL_HEADS
    p2v = p2.reshape(B, S, p2.shape[1])
    g1v = g1.reshape(B, 4, S // 4, GROUP_W)
    g2v = g2.reshape(B, 16, S // 16, GROUP_W)
    b0 = off0 // LANES
    zb0 = zb_off // LANES

    def nat(cb):
        return pl.BlockSpec((None, S, LANES), lambda b, h: (b, 0, cb + h))

    def perm(d, cb):
        return pl.BlockSpec((None, d, S // d, LANES), lambda b, h: (b, 0, 0, cb + h))

    return pl.pallas_call(
        _attn_kernel,
        grid=(B, H),
        in_specs=[nat(b0), nat(b0 + H), nat(b0 + 2 * H),
                  perm(4, 0), perm(4, H), perm(4, 2 * H),
                  perm(16, 0), perm(16, H), perm(16, 2 * H),
                  nat(zb0)],
        out_specs=pl.BlockSpec((None, S, LANES), lambda b, h: (b, 0, h)),
        out_shape=jax.ShapeDtypeStruct((B, S, DIL_W), BF16),
        scratch_shapes=[pltpu.VMEM((S, LANES), F32)] * 3,
        compiler_params=pltpu.CompilerParams(dimension_semantics=("arbitrary", "arbitrary"),
                                             vmem_limit_bytes=VMEM_LIMIT),
        name="dilated_attention",
    )(p2v, p2v, p2v, g1v, g1v, g1v, g2v, g2v, g2v, p2v)


def _out_kernel(oa_ref, ob_ref, ga_ref, gb_ref, x_ref, wa_ref, wb_ref, wo_ref, fw_ref, o_ref, *, final):
    ya = jnp.dot(oa_ref[...], wa_ref[...], preferred_element_type=F32)
    yb = jnp.dot(ob_ref[...], wb_ref[...], preferred_element_type=F32)
    merged = ga_ref[...].astype(F32) * ya + gb_ref[...].astype(F32) * yb
    xn = x_ref[...] + jnp.dot(merged.astype(BF16), wo_ref[...], preferred_element_type=F32)
    if final:
        xn = xn * lax.rsqrt(jnp.mean(xn * xn, axis=-1, keepdims=True) + NORM_EPS) * fw_ref[...]
    o_ref[...] = xn


def _out_call(oa, ob, p2, ga_off, gb_off, x2d, wa, wb, wo, fw, final):
    T, D = x2d.shape
    tm = OUT_TM
    ga_b = ga_off // D
    gb_b = gb_off // D
    const = lambda i: (0, 0)
    return pl.pallas_call(
        functools.partial(_out_kernel, final=final),
        grid=(T // tm,),
        in_specs=[pl.BlockSpec((tm, DN_W), lambda i: (i, 0)),
                  pl.BlockSpec((tm, DIL_W), lambda i: (i, 0)),
                  pl.BlockSpec((tm, D), lambda i: (i, ga_b)),
                  pl.BlockSpec((tm, D), lambda i: (i, gb_b)),
                  pl.BlockSpec((tm, D), lambda i: (i, 0)),
                  pl.BlockSpec((DN_W, D), const),
                  pl.BlockSpec((DIL_W, D), const),
                  pl.BlockSpec((D, D), const),
                  pl.BlockSpec((1, D), const)],
        out_specs=pl.BlockSpec((tm, D), lambda i: (i, 0)),
        out_shape=jax.ShapeDtypeStruct((T, D), F32),
        compiler_params=pltpu.CompilerParams(dimension_semantics=("arbitrary",),
                                             vmem_limit_bytes=VMEM_LIMIT),
        name="out_merge",
    )(oa, ob, p2, p2, x2d, wa, wb, wo, fw)


def _split_cols(w):
    out, start = [], 0
    for s in PROJ_SIZES:
        out.append(w[:, start:start + s])
        start += s
    return out


def _layer(x, norm_w, w_in, conv_w, a_log, dt_bias, dn_norm_w, w_o_dn, w_o_dil, w_out, final_w, final):
    B, S, D = x.shape
    T = B * S
    (wq_a, wk_a, wv_a, wz_a, wb_a, wa_a, wq_b, wk_b, wv_b, wz_b, wg_a, wg_b) = _split_cols(w_in)
    small = jnp.concatenate([wb_a, wa_a, jnp.zeros((D, SMALL_PAD - 2 * DN_HEADS), w_in.dtype)], axis=1)
    w1 = jnp.concatenate([wq_a, wk_a, wv_a, small], axis=1).astype(BF16)

    def group_w(g):
        sl = slice(g * DIL_W, (g + 1) * DIL_W)
        return jnp.concatenate([wq_b[:, sl], wk_b[:, sl], wv_b[:, sl]], axis=1)

    w2 = jnp.concatenate([wz_a, wz_b, group_w(0), wg_a, wg_b], axis=1).astype(BF16)
    zb_off = DN_W
    off0 = zb_off + DIL_W
    ga_off = off0 + GROUP_W
    gb_off = ga_off + D
    tn = PROJ_TN
    segs2 = ((0, off0 // tn, "silu"), (off0 // tn, ga_off // tn, "none"), (ga_off // tn, (gb_off + D) // tn, "sigmoid"))

    h, h4, h16 = _norm_call(x, norm_w.astype(F32))
    p1 = _proj_call(h.reshape(T, D), w1, ((0, w1.shape[1] // tn, "none"),), F32, "proj_deltanet")
    p2 = _proj_call(h.reshape(T, D), w2, segs2, BF16, "proj_gates_group0")
    g1 = _proj_call(h4.reshape(T, D), group_w(1).astype(BF16), ((0, GROUP_W // tn, "none"),), BF16, "proj_group1")
    g2 = _proj_call(h16.reshape(T, D), group_w(2).astype(BF16), ((0, GROUP_W // tn, "none"),), BF16, "proj_group2")

    o_a = _deltanet_call(p1, p2, conv_w, a_log, dt_bias, dn_norm_w, B, S)
    o_b = _attn_call(p2, off0, zb_off, g1, g2, B, S)
    out = _out_call(o_a, o_b.reshape(T, DIL_W), p2, ga_off, gb_off, x.reshape(T, D),
                    w_o_dn.astype(BF16), w_o_dil.astype(BF16), w_out.astype(BF16),
                    final_w.astype(F32).reshape(1, D), final)
    return out.reshape(B, S, D)


def kernel(x, norm_w, w_in, conv_w, a_log, dt_bias, dn_norm_w, w_o_dn, w_o_dil, w_out, final_norm_w):
    depth = norm_w.shape[0]
    for l in range(depth):
        x = _layer(x, norm_w[l], w_in[l], conv_w[l], a_log[l], dt_bias[l], dn_norm_w[l],
                   w_o_dn[l], w_o_dil[l], w_out[l], final_norm_w, final=(l == depth - 1))
    return x
```

```python
import functools
import math

import jax
import jax.numpy as jnp
import numpy as np
from jax import lax
from jax.experimental import pallas as pl
from jax.experimental.pallas import tpu as pltpu

F32 = jnp.float32
BF16 = jnp.bfloat16

D_MODEL = 1024
DN_HEADS = 8
DN_DK = 128
DN_DV = 128
DN_CONV = 4
DN_CHUNK = 64
DIL_GROUPS = ((128, 1), (512, 4), (2048, 16))
DIL_HEADS = 4
DIL_DH = 128
ATT_BLOCK = 128
NORM_EPS = 1e-6
N_DIL = len(DIL_GROUPS)
DN_W = DN_HEADS * DN_DK
DIL_W = DIL_HEADS * DIL_DH
GROUP_W = 3 * DIL_W
PROJ_SIZES = (DN_W, DN_W, DN_W, DN_W, DN_HEADS, DN_HEADS,
              N_DIL * DIL_W, N_DIL * DIL_W, N_DIL * DIL_W, DIL_W, D_MODEL, D_MODEL)

LANES = 128
SMALL_PAD = 512
VMEM_LIMIT = 56 * 1024 * 1024

NORM_TM = 512
PROJ_TM = 2048
PROJ_TN = 512
DN_TB = 128
OUT_TM = 1024


def _sigmoid(x):
    return 1.0 / (1.0 + jnp.exp(-x))


def _silu(x):
    return x * _sigmoid(x)


def _softplus(x):
    return jnp.maximum(x, 0.0) + jnp.log1p(jnp.exp(-jnp.abs(x)))


def _mm(a, b):
    return lax.dot_general(a.astype(BF16), b.astype(BF16), (((1,), (0,)), ((), ())),
                           preferred_element_type=F32)


def _mm_nt(a, b):
    return lax.dot_general(a.astype(BF16), b.astype(BF16), (((1,), (1,)), ((), ())),
                           preferred_element_type=F32)


def _mm_tn(a, b):
    return lax.dot_general(a.astype(BF16), b.astype(BF16), (((0,), (0,)), ((), ())),
                           preferred_element_type=F32)


def _mm_f32(a, b):
    return lax.dot_general(a, b, (((1,), (0,)), ((), ())), precision=lax.Precision.HIGHEST,
                           preferred_element_type=F32)


def _norm_kernel(x_ref, w_ref, h_ref, h4_ref, h16_ref, slab_ref):
    x = x_ref[...]
    y = x * lax.rsqrt(jnp.mean(x * x, axis=-1, keepdims=True) + NORM_EPS) * w_ref[...]
    h_ref[...] = y.astype(BF16)
    n_slabs = D_MODEL // LANES
    for c in range(n_slabs):
        slab_ref[c] = y[:, c * LANES:(c + 1) * LANES]
    tm = x.shape[0]
    for d, out in ((4, h4_ref), (16, h16_ref)):
        n = tm // d
        for r in range(d):
            for c in range(n_slabs):
                out[r, :, c * LANES:(c + 1) * LANES] = slab_ref[c, pl.ds(r, n, stride=d), :].astype(BF16)


def _norm_call(x, w):
    B, S, D = x.shape
    tm = NORM_TM
    nt = S // tm
    return pl.pallas_call(
        _norm_kernel,
        grid=(B, nt),
        in_specs=[pl.BlockSpec((None, tm, D), lambda b, i: (b, i, 0)),
                  pl.BlockSpec((1, D), lambda b, i: (0, 0))],
        out_specs=[pl.BlockSpec((None, tm, D), lambda b, i: (b, i, 0)),
                   pl.BlockSpec((None, 4, tm // 4, D), lambda b, i: (b, 0, i, 0)),
                   pl.BlockSpec((None, 16, tm // 16, D), lambda b, i: (b, 0, i, 0))],
        out_shape=[jax.ShapeDtypeStruct((B, S, D), BF16),
                   jax.ShapeDtypeStruct((B, 4, S // 4, D), BF16),
                   jax.ShapeDtypeStruct((B, 16, S // 16, D), BF16)],
        scratch_shapes=[pltpu.VMEM((D // LANES, tm, LANES), F32)],
        compiler_params=pltpu.CompilerParams(dimension_semantics=("arbitrary", "arbitrary"),
                                             vmem_limit_bytes=VMEM_LIMIT),
        name="norm_deinterleave",
    )(x, w.reshape(1, D))


_ACTS = {"none": lambda v: v, "silu": _silu, "sigmoid": _sigmoid}


def _proj_kernel(h_ref, w_ref, o_ref, *, segs):
    acc = jnp.dot(h_ref[...], w_ref[...], preferred_element_type=F32)
    if len(segs) == 1:
        o_ref[...] = _ACTS[segs[0][2]](acc).astype(o_ref.dtype)
        return
    j = pl.program_id(1)
    for lo, hi, act in segs:
        @pl.when((j >= lo) & (j < hi))
        def _(act=act):
            o_ref[...] = _ACTS[act](acc).astype(o_ref.dtype)


def _proj_call(h2d, w, segs, out_dtype, name):
    T, D = h2d.shape
    N = w.shape[1]
    tm, tn = PROJ_TM, PROJ_TN
    return pl.pallas_call(
        functools.partial(_proj_kernel, segs=segs),
        grid=(T // tm, N // tn),
        in_specs=[pl.BlockSpec((tm, D), lambda i, j: (i, 0)),
                  pl.BlockSpec((D, tn), lambda i, j: (0, j))],
        out_specs=pl.BlockSpec((tm, tn), lambda i, j: (i, j)),
        out_shape=jax.ShapeDtypeStruct((T, N), out_dtype),
        compiler_params=pltpu.CompilerParams(dimension_semantics=("arbitrary", "arbitrary"),
                                             vmem_limit_bytes=VMEM_LIMIT),
        name=name,
    )(h2d, w)


def _deltanet_kernel(qkv_ref, z_ref, cw_ref, hp_ref, nw_ref, o_ref, xs_ref, st_ref):
    C = DN_CHUNK
    tb = o_ref.shape[0]
    qkv_w = 3 * DN_W
    i = pl.program_id(1)

    @pl.when(i == 0)
    def _():
        st_ref[...] = jnp.zeros_like(st_ref)
        xs_ref[0:8, :] = jnp.zeros((8, qkv_w), F32)

    @pl.when(i > 0)
    def _():
        xs_ref[0:8, :] = xs_ref[tb:tb + 8, :]

    xs_ref[8:tb + 8, :] = qkv_ref[:, 0:qkv_w]

    row = lax.broadcasted_iota(jnp.int32, (C, C), 0)
    col = lax.broadcasted_iota(jnp.int32, (C, C), 1)
    causal = row >= col
    strict = row > col
    ltri = causal.astype(F32)
    hp = hp_ref[...]
    neg_a = -jnp.exp(hp[0:1, :])
    dt_b = hp[1:2, :]
    nw = nw_ref[...]

    def conv_silu(base, c0):
        u = xs_ref[base:base + C + 8, c0:c0 + LANES]
        acc = cw_ref[DN_CONV - 1:DN_CONV, c0:c0 + LANES] * u[8:, :]
        for k in range(DN_CONV - 1):
            acc = acc + cw_ref[k:k + 1, c0:c0 + LANES] * pltpu.roll(u, DN_CONV - 1 - k, axis=0)[8:, :]
        return _silu(acc)

    heads = range(DN_HEADS)
    for c in range(tb // C):
        base = c * C
        sm = qkv_ref[base:base + C, qkv_w:qkv_w + LANES]
        beta_all = _sigmoid(sm)
        g_all = neg_a * _softplus(sm + dt_b)
        gc_all = _mm_f32(ltri, g_all)
        gc_t = jnp.concatenate([gc_all, jnp.zeros_like(gc_all)], axis=0).T
        glast_all = gc_all[C - 1:C, :]
        eg_all = jnp.exp(gc_all)
        ek_all = jnp.exp(glast_all - gc_all)
        dl_all = jnp.exp(glast_all)

        def bcast(a, lane):
            return jnp.broadcast_to(a[:, lane:lane + 1], (C, LANES))

        q, k, v, kb, eg, gamma = [], [], [], [], [], []
        for h in heads:
            lane = DN_HEADS + h
            qh = conv_silu(base, h * DN_DK)
            kh = conv_silu(base, DN_W + h * DN_DK)
            vh = conv_silu(base, 2 * DN_W + h * DN_DV)
            qh = qh * (lax.rsqrt(jnp.sum(qh * qh, axis=-1, keepdims=True) + NORM_EPS) * (DN_DK ** -0.5))
            kh = kh * lax.rsqrt(jnp.sum(kh * kh, axis=-1, keepdims=True) + NORM_EPS)
            beta = bcast(beta_all, h)
            q.append(qh)
            k.append(kh)
            v.append(vh * beta)
            kb.append(kh * beta)
            eg.append(bcast(eg_all, lane))
            diff = bcast(gc_all, lane)[:, 0:C] - gc_t[lane:lane + 1, 0:C]
            gamma.append(jnp.exp(jnp.where(causal, diff, -jnp.inf)))

        a_kk = [_mm_nt(kb[h], k[h]) for h in heads]
        a_qk = [_mm_nt(q[h], k[h]) for h in heads]
        p = [jnp.where(strict, -(a_kk[h] * gamma[h]), 0.0) for h in heads]
        x = [jnp.concatenate([v[h], kb[h] * eg[h]], axis=1) for h in heads]
        aqk = [a_qk[h] * gamma[h] for h in heads]
        x = [x[h] + _mm(p[h], x[h]) for h in heads]
        for _ in range(5):
            p = [_mm(p[h], p[h]) for h in heads]
            x = [x[h] + _mm(p[h], x[h]) for h in heads]

        s = [st_ref[h] for h in heads]
        ws_qs = [_mm(jnp.concatenate([x[h][:, DN_DV:], q[h] * eg[h]], axis=0), s[h]) for h in heads]
        v_new = [x[h][:, 0:DN_DV] - ws_qs[h][0:C] for h in heads]
        o = [ws_qs[h][C:] + _mm(aqk[h], v_new[h]) for h in heads]
        for h in heads:
            lane = DN_HEADS + h
            kd = k[h] * bcast(ek_all, lane)
            dl = jnp.broadcast_to(dl_all[:, lane:lane + 1], (DN_DK, DN_DV))
            st_ref[h] = s[h] * dl + _mm_tn(kd, v_new[h])
        for h in heads:
            y = o[h] * lax.rsqrt(jnp.mean(o[h] * o[h], axis=-1, keepdims=True) + NORM_EPS) * nw
            gate = z_ref[base:base + C, h * DN_DV:(h + 1) * DN_DV].astype(F32)
            o_ref[base:base + C, h * DN_DV:(h + 1) * DN_DV] = (y * gate).astype(o_ref.dtype)


def _deltanet_call(p1, p2, conv_w, a_log, dt_bias, dn_norm_w, B, S):
    tb = DN_TB
    nt = S // tb
    w1 = p1.shape[1]
    cw = jnp.zeros((8, 3 * DN_W), F32).at[0:DN_CONV].set(conv_w.astype(F32))
    hp = jnp.zeros((8, LANES), F32)
    hp = hp.at[0, DN_HEADS:2 * DN_HEADS].set(a_log.astype(F32)).at[1, DN_HEADS:2 * DN_HEADS].set(dt_bias.astype(F32))
    nw = dn_norm_w.astype(F32).reshape(1, DN_DV)
    return pl.pallas_call(
        _deltanet_kernel,
        grid=(B, nt),
        in_specs=[pl.BlockSpec((tb, w1), lambda b, i: (b * nt + i, 0)),
                  pl.BlockSpec((tb, DN_W), lambda b, i: (b * nt + i, 0)),
                  pl.BlockSpec((8, 3 * DN_W), lambda b, i: (0, 0)),
                  pl.BlockSpec((8, LANES), lambda b, i: (0, 0)),
                  pl.BlockSpec((1, DN_DV), lambda b, i: (0, 0))],
        out_specs=pl.BlockSpec((tb, DN_W), lambda b, i: (b * nt + i, 0)),
        out_shape=jax.ShapeDtypeStruct((B * S, DN_W), BF16),
        scratch_shapes=[pltpu.VMEM((tb + 8, 3 * DN_W), F32),
                        pltpu.VMEM((DN_HEADS, DN_DK, DN_DV), F32)],
        compiler_params=pltpu.CompilerParams(dimension_semantics=("arbitrary", "arbitrary"),
                                             vmem_limit_bytes=VMEM_LIMIT),
        name="deltanet",
    )(p1, p2, cw, hp, nw)


def _alibi_slope(idx):
    n = N_DIL * DIL_HEADS
    return 2.0 ** (-8.0 * (idx + 1) / n)


def _attn_kernel(q0_ref, k0_ref, v0_ref, q1_ref, k1_ref, v1_ref, q2_ref, k2_ref, v2_ref, z_ref, o_ref,
                 num_ref, den_ref, mx_ref):
    Q = ATT_BLOCK
    hd = pl.program_id(1)
    span = DIL_GROUPS[0][0] // DIL_GROUPS[0][1]
    dist = (Q + lax.broadcasted_iota(jnp.int32, (Q, 2 * Q), 0) - lax.broadcasted_iota(jnp.int32, (Q, 2 * Q), 1))
    valid = (dist >= 0) & (dist <= span)
    distf = dist.astype(F32)
    scale = DIL_DH ** -0.5

    def slope_of(g):
        s = jnp.float32(_alibi_slope(g * DIL_HEADS))
        for h in range(1, DIL_HEADS):
            s = jnp.where(hd == h, jnp.float32(_alibi_slope(g * DIL_HEADS + h)), s)
        return s

    def block(q, k, v, bias):
        s = _mm_nt(q, k) * scale + bias
        mx = jnp.max(s, axis=-1, keepdims=True)
        p = jnp.exp(s - mx)
        den = jnp.sum(p, axis=-1, keepdims=True)
        num = _mm(p, v)
        return num, jnp.broadcast_to(den, (Q, LANES)), jnp.broadcast_to(mx, (Q, LANES))

    def merge(a, b):
        (n1, d1, m1), (n2, d2, m2) = a, b
        m = jnp.maximum(m1, m2)
        a1 = jnp.exp(m1 - m)
        a2 = jnp.exp(m2 - m)
        return n1 * a1 + n2 * a2, d1 * a1 + d2 * a2, m

    def run_group(g, q_ref, k_ref, v_ref, emit):
        d = DIL_GROUPS[g][1]
        bias2 = jnp.where(valid, -(slope_of(g) * float(d)) * distf, -jnp.inf)
        bias1 = bias2[:, Q:]
        nb = q_ref.shape[-2] // Q

        def sub(r, carry):
            if d == 1:
                qr, kr, vr = q_ref, k_ref, v_ref
            else:
                qr, kr, vr = q_ref.at[r], k_ref.at[r], v_ref.at[r]
            emit(r, 0, block(qr[0:Q, :], kr[0:Q, :], vr[0:Q, :], bias1))

            def qblock(j, c2):
                lo = pl.multiple_of((j - 1) * Q, Q)
                cur = pl.multiple_of(j * Q, Q)
                emit(r, j, block(qr[pl.ds(cur, Q), :], kr[pl.ds(lo, 2 * Q), :], vr[pl.ds(lo, 2 * Q), :], bias2))
                return c2

            lax.fori_loop(1, nb, qblock, 0)
            return carry

        if d == 1:
            sub(0, 0)
        else:
            lax.fori_loop(0, d, sub, 0)

    def rows(g, r, j):
        d = DIL_GROUPS[g][1]
        if d == 1:
            return pl.ds(j * Q, Q) if isinstance(j, int) else pl.ds(pl.multiple_of(j * Q, Q), Q)
        return pl.ds(j * (Q * d) + r, Q, stride=d)

    def emit_first(r, j, res):
        idx = rows(2, r, j)
        num_ref[idx, :] = res[0]
        den_ref[idx, :] = res[1]
        mx_ref[idx, :] = res[2]

    def emit_second(r, j, res):
        idx = rows(1, r, j)
        n, dn, m = merge((num_ref[idx, :], den_ref[idx, :], mx_ref[idx, :]), res)
        num_ref[idx, :] = n
        den_ref[idx, :] = dn
        mx_ref[idx, :] = m

    def emit_last(r, j, res):
        idx = rows(0, r, j)
        n, dn, _ = merge((num_ref[idx, :], den_ref[idx, :], mx_ref[idx, :]), res)
        o_ref[idx, :] = ((n / dn) * z_ref[idx, :].astype(F32)).astype(o_ref.dtype)

    run_group(2, q2_ref, k2_ref, v2_ref, emit_first)
    run_group(1, q1_ref, k1_ref, v1_ref, emit_second)
    run_group(0, q0_ref, k0_ref, v0_ref, emit_last)


def _attn_call(p2, off0, zb_off, g1, g2, B, S):
    H = DIL_HEADS
    p2v = p2.reshape(B, S, p2.shape[1])
    g1v = g1.reshape(B, 4, S // 4, GROUP_W)
    g2v = g2.reshape(B, 16, S // 16, GROUP_W)
    b0 = off0 // LANES
    zb0 = zb_off // LANES

    def nat(cb):
        return pl.BlockSpec((None, S, LANES), lambda b, h: (b, 0, cb + h))

    def perm(d, cb):
        return pl.BlockSpec((None, d, S // d, LANES), lambda b, h: (b, 0, 0, cb + h))

    return pl.pallas_call(
        _attn_kernel,
        grid=(B, H),
        in_specs=[nat(b0), nat(b0 + H), nat(b0 + 2 * H),
                  perm(4, 0), perm(4, H), perm(4, 2 * H),
                  perm(16, 0), perm(16, H), perm(16, 2 * H),
                  nat(zb0)],
        out_specs=pl.BlockSpec((None, S, LANES), lambda b, h: (b, 0, h)),
        out_shape=jax.ShapeDtypeStruct((B, S, DIL_W), BF16),
        scratch_shapes=[pltpu.VMEM((S, LANES), F32)] * 3,
        compiler_params=pltpu.CompilerParams(dimension_semantics=("arbitrary", "arbitrary"),
                                             vmem_limit_bytes=VMEM_LIMIT),
        name="dilated_attention",
    )(p2v, p2v, p2v, g1v, g1v, g1v, g2v, g2v, g2v, p2v)


def _out_kernel(oa_ref, ob_ref, ga_ref, gb_ref, x_ref, wa_ref, wb_ref, wo_ref, fw_ref, o_ref, *, final):
    ya = jnp.dot(oa_ref[...], wa_ref[...], preferred_element_type=F32)
    yb = jnp.dot(ob_ref[...], wb_ref[...], preferred_element_type=F32)
    merged = ga_ref[...].astype(F32) * ya + gb_ref[...].astype(F32) * yb
    xn = x_ref[...] + jnp.dot(merged.astype(BF16), wo_ref[...], preferred_element_type=F32)
    if final:
        xn = xn * lax.rsqrt(jnp.mean(xn * xn, axis=-1, keepdims=True) + NORM_EPS) * fw_ref[...]
    o_ref[...] = xn


def _out_call(oa, ob, p2, ga_off, gb_off, x2d, wa, wb, wo, fw, final):
    T, D = x2d.shape
    tm = OUT_TM
    ga_b = ga_off // D
    gb_b = gb_off // D
    const = lambda i: (0, 0)
    return pl.pallas_call(
        functools.partial(_out_kernel, final=final),
        grid=(T // tm,),
        in_specs=[pl.BlockSpec((tm, DN_W), lambda i: (i, 0)),
                  pl.BlockSpec((tm, DIL_W), lambda i: (i, 0)),
                  pl.BlockSpec((tm, D), lambda i: (i, ga_b)),
                  pl.BlockSpec((tm, D), lambda i: (i, gb_b)),
                  pl.BlockSpec((tm, D), lambda i: (i, 0)),
                  pl.BlockSpec((DN_W, D), const),
                  pl.BlockSpec((DIL_W, D), const),
                  pl.BlockSpec((D, D), const),
                  pl.BlockSpec((1, D), const)],
        out_specs=pl.BlockSpec((tm, D), lambda i: (i, 0)),
        out_shape=jax.ShapeDtypeStruct((T, D), F32),
        compiler_params=pltpu.CompilerParams(dimension_semantics=("arbitrary",),
                                             vmem_limit_bytes=VMEM_LIMIT),
        name="out_merge",
    )(oa, ob, p2, p2, x2d, wa, wb, wo, fw)


def _split_cols(w):
    out, start = [], 0
    for s in PROJ_SIZES:
        out.append(w[:, start:start + s])
        start += s
    return out


def _layer(x, norm_w, w_in, conv_w, a_log, dt_bias, dn_norm_w, w_o_dn, w_o_dil, w_out, final_w, final):
    B, S, D = x.shape
    T = B * S
    (wq_a, wk_a, wv_a, wz_a, wb_a, wa_a, wq_b, wk_b, wv_b, wz_b, wg_a, wg_b) = _split_cols(w_in)
    small = jnp.concatenate([wb_a, wa_a, jnp.zeros((D, SMALL_PAD - 2 * DN_HEADS), w_in.dtype)], axis=1)
    w1 = jnp.concatenate([wq_a, wk_a, wv_a, small], axis=1).astype(BF16)

    def group_w(g):
        sl = slice(g * DIL_W, (g + 1) * DIL_W)
        return jnp.concatenate([wq_b[:, sl], wk_b[:, sl], wv_b[:, sl]], axis=1)

    w2 = jnp.concatenate([wz_a, wz_b, group_w(0), wg_a, wg_b], axis=1).astype(BF16)
    zb_off = DN_W
    off0 = zb_off + DIL_W
    ga_off = off0 + GROUP_W
    gb_off = ga_off + D
    tn = PROJ_TN
    segs2 = ((0, off0 // tn, "silu"), (off0 // tn, ga_off // tn, "none"), (ga_off // tn, (gb_off + D) // tn, "sigmoid"))

    h, h4, h16 = _norm_call(x, norm_w.astype(F32))
    p1 = _proj_call(h.reshape(T, D), w1, ((0, w1.shape[1] // tn, "none"),), F32, "proj_deltanet")
    p2 = _proj_call(h.reshape(T, D), w2, segs2, BF16, "proj_gates_group0")
    g1 = _proj_call(h4.reshape(T, D), group_w(1).astype(BF16), ((0, GROUP_W // tn, "none"),), BF16, "proj_group1")
    g2 = _proj_call(h16.reshape(T, D), group_w(2).astype(BF16), ((0, GROUP_W // tn, "none"),), BF16, "proj_group2")

    o_a = _deltanet_call(p1, p2, conv_w, a_log, dt_bias, dn_norm_w, B, S)
    o_b = _attn_call(p2, off0, zb_off, g1, g2, B, S)
    out = _out_call(o_a, o_b.reshape(T, DIL_W), p2, ga_off, gb_off, x.reshape(T, D),
                    w_o_dn.astype(BF16), w_o_dil.astype(BF16), w_out.astype(BF16),
                    final_w.astype(F32).reshape(1, D), final)
    return out.reshape(B, S, D)


def kernel(x, norm_w, w_in, conv_w, a_log, dt_bias, dn_norm_w, w_o_dn, w_o_dil, w_out, final_norm_w):
    depth = norm_w.shape[0]
    for l in range(depth):
        x = _layer(x, norm_w[l], w_in[l], conv_w[l], a_log[l], dt_bias[l], dn_norm_w[l],
                   w_o_dn[l], w_o_dil[l], w_out[l], final_norm_w, final=(l == depth - 1))
    return x
```

```python
import functools
import math

import jax
import jax.numpy as jnp
import numpy as np
from jax import lax
from jax.experimental import pallas as pl
from jax.experimental.pallas import tpu as pltpu

F32 = jnp.float32
BF16 = jnp.bfloat16

D_MODEL = 1024
DN_HEADS = 8
DN_DK = 128
DN_DV = 128
DN_CONV = 4
DN_CHUNK = 64
DIL_GROUPS = ((128, 1), (512, 4), (2048, 16))
DIL_HEADS = 4
DIL_DH = 128
ATT_BLOCK = 128
NORM_EPS = 1e-6
N_DIL = len(DIL_GROUPS)
DN_W = DN_HEADS * DN_DK
DIL_W = DIL_HEADS * DIL_DH
GROUP_W = 3 * DIL_W
PROJ_SIZES = (DN_W, DN_W, DN_W, DN_W, DN_HEADS, DN_HEADS,
              N_DIL * DIL_W, N_DIL * DIL_W, N_DIL * DIL_W, DIL_W, D_MODEL, D_MODEL)

LANES = 128
VMEM_LIMIT = 56 * 1024 * 1024

NORM_TM = 512
PROJ_TM = 2048
PROJ_TN = 512
PROJ_SUB = 256
DN_TB = 128
OUT_TM = 1024


def _sigmoid(x):
    return 1.0 / (1.0 + jnp.exp(-x))


def _silu(x):
    return x * _sigmoid(x)


def _softplus(x):
    return jnp.maximum(x, 0.0) + jnp.log1p(jnp.exp(-jnp.abs(x)))


def _mm(a, b):
    return lax.dot_general(a.astype(BF16), b.astype(BF16), (((1,), (0,)), ((), ())),
                           preferred_element_type=F32)


def _mm_nt(a, b):
    return lax.dot_general(a.astype(BF16), b.astype(BF16), (((1,), (1,)), ((), ())),
                           preferred_element_type=F32)


def _mm_tn(a, b):
    return lax.dot_general(a.astype(BF16), b.astype(BF16), (((0,), (0,)), ((), ())),
                           preferred_element_type=F32)


def _mm_f32(a, b):
    return lax.dot_general(a, b, (((1,), (0,)), ((), ())), precision=lax.Precision.HIGHEST,
                           preferred_element_type=F32)


def _norm_kernel(x_ref, w_ref, ws_ref, h_ref, h4_ref, h16_ref, sm_ref, slab_ref):
    x = x_ref[...]
    y = x * lax.rsqrt(jnp.mean(x * x, axis=-1, keepdims=True) + NORM_EPS) * w_ref[...]
    hb = y.astype(BF16)
    h_ref[...] = hb
    sm_ref[...] = jnp.dot(hb, ws_ref[...], preferred_element_type=F32)
    n_slabs = D_MODEL // LANES
    for c in range(n_slabs):
        slab_ref[c] = y[:, c * LANES:(c + 1) * LANES]
    tm = x.shape[0]
    for d, out in ((4, h4_ref), (16, h16_ref)):
        n = tm // d
        for r in range(d):
            for c in range(n_slabs):
                out[r, :, c * LANES:(c + 1) * LANES] = slab_ref[c, pl.ds(r, n, stride=d), :].astype(BF16)


def _norm_call(x, w, w_small):
    B, S, D = x.shape
    tm = NORM_TM
    nt = S // tm
    return pl.pallas_call(
        _norm_kernel,
        grid=(B, nt),
        in_specs=[pl.BlockSpec((None, tm, D), lambda b, i: (b, i, 0)),
                  pl.BlockSpec((1, D), lambda b, i: (0, 0)),
                  pl.BlockSpec((D, LANES), lambda b, i: (0, 0))],
        out_specs=[pl.BlockSpec((None, tm, D), lambda b, i: (b, i, 0)),
                   pl.BlockSpec((None, 4, tm // 4, D), lambda b, i: (b, 0, i, 0)),
                   pl.BlockSpec((None, 16, tm // 16, D), lambda b, i: (b, 0, i, 0)),
                   pl.BlockSpec((None, tm, LANES), lambda b, i: (b, i, 0))],
        out_shape=[jax.ShapeDtypeStruct((B, S, D), BF16),
                   jax.ShapeDtypeStruct((B, 4, S // 4, D), BF16),
                   jax.ShapeDtypeStruct((B, 16, S // 16, D), BF16),
                   jax.ShapeDtypeStruct((B, S, LANES), F32)],
        scratch_shapes=[pltpu.VMEM((D // LANES, tm, LANES), F32)],
        compiler_params=pltpu.CompilerParams(dimension_semantics=("arbitrary", "arbitrary"),
                                             vmem_limit_bytes=VMEM_LIMIT),
        name="norm_deinterleave",
    )(x, w.reshape(1, D), w_small)


_ACTS = {"none": lambda v: v, "silu": _silu, "sigmoid": _sigmoid}


def _proj_kernel(h_ref, w_ref, o_ref, *, segs):
    acc = jnp.dot(h_ref[...], w_ref[...], preferred_element_type=F32)
    if len(segs) == 1:
        o_ref[...] = _ACTS[segs[0][2]](acc).astype(o_ref.dtype)
        return
    j = pl.program_id(1)
    for lo, hi, act in segs:
        @pl.when((j >= lo) & (j < hi))
        def _(act=act):
            o_ref[...] = _ACTS[act](acc).astype(o_ref.dtype)


def _proj_call(h2d, w, segs, out_dtype, name):
    T, D = h2d.shape
    N = w.shape[1]
    tm, tn = PROJ_TM, PROJ_TN
    return pl.pallas_call(
        functools.partial(_proj_kernel, segs=segs),
        grid=(T // tm, N // tn),
        in_specs=[pl.BlockSpec((tm, D), lambda i, j: (i, 0)),
                  pl.BlockSpec((D, tn), lambda i, j: (0, j))],
        out_specs=pl.BlockSpec((tm, tn), lambda i, j: (i, j)),
        out_shape=jax.ShapeDtypeStruct((T, N), out_dtype),
        compiler_params=pltpu.CompilerParams(dimension_semantics=("arbitrary", "arbitrary"),
                                             vmem_limit_bytes=VMEM_LIMIT),
        name=name,
    )(h2d, w)


def _proj_conv_kernel(h_ref, w_ref, cw_ref, o_ref, tail_ref, *, tiles_per_seq):
    i = pl.program_id(1)
    tm = h_ref.shape[0]
    sub = PROJ_SUB

    @pl.when(i % tiles_per_seq == 0)
    def _():
        tail_ref[...] = jnp.zeros_like(tail_ref)

    prev = tail_ref[...]
    taps = [cw_ref[k:k + 1, :] for k in range(DN_CONV)]
    for m in range(tm // sub):
        acc = jnp.dot(h_ref[m * sub:(m + 1) * sub, :], w_ref[...], preferred_element_type=F32)
        ext = jnp.concatenate([prev, acc], axis=0)
        y = taps[DN_CONV - 1] * acc
        for k in range(DN_CONV - 1):
            y = y + taps[k] * pltpu.roll(ext, DN_CONV - 1 - k, axis=0)[8:, :]
        o_ref[m * sub:(m + 1) * sub, :] = _silu(y).astype(o_ref.dtype)
        prev = acc[sub - 8:sub, :]
    tail_ref[...] = prev


def _proj_conv_call(h2d, w, conv_w, S, name):
    T, D = h2d.shape
    N = w.shape[1]
    tm, tn = PROJ_TM, PROJ_TN
    cw = jnp.zeros((8, N), F32).at[0:DN_CONV].set(conv_w.astype(F32))
    return pl.pallas_call(
        functools.partial(_proj_conv_kernel, tiles_per_seq=S // tm),
        grid=(N // tn, T // tm),
        in_specs=[pl.BlockSpec((tm, D), lambda j, i: (i, 0)),
                  pl.BlockSpec((D, tn), lambda j, i: (0, j)),
                  pl.BlockSpec((8, tn), lambda j, i: (0, j))],
        out_specs=pl.BlockSpec((tm, tn), lambda j, i: (i, j)),
        out_shape=jax.ShapeDtypeStruct((T, N), BF16),
        scratch_shapes=[pltpu.VMEM((8, tn), F32)],
        compiler_params=pltpu.CompilerParams(dimension_semantics=("arbitrary", "arbitrary"),
                                             vmem_limit_bytes=VMEM_LIMIT),
        name=name,
    )(h2d, w, cw)


def _deltanet_kernel(qkv_ref, sm_ref, z_ref, hp_ref, nw_ref, o_ref, st_ref):
    C = DN_CHUNK
    tb = o_ref.shape[0]

    @pl.when(pl.program_id(1) == 0)
    def _():
        st_ref[...] = jnp.zeros_like(st_ref)

    row = lax.broadcasted_iota(jnp.int32, (C, C), 0)
    col = lax.broadcasted_iota(jnp.int32, (C, C), 1)
    causal = row >= col
    strict = row > col
    ltri = causal.astype(F32)
    eye = (row == col).astype(F32)
    hp = hp_ref[...]
    neg_a = -jnp.exp(hp[0:1, :])
    dt_b = hp[1:2, :]
    nw = nw_ref[...]

    heads = range(DN_HEADS)
    chunks = range(tb // C)

    def bcast(a, lane):
        return jnp.broadcast_to(a[:, lane:lane + 1], (C, LANES))

    def load(base, c0):
        return qkv_ref[base:base + C, c0:c0 + LANES].astype(F32)

    dec = []
    for c in chunks:
        sm = sm_ref[c * C:(c + 1) * C, :]
        g_all = neg_a * _softplus(sm + dt_b)
        gc_all = _mm_f32(ltri, g_all)
        gc_t = jnp.concatenate([gc_all, jnp.zeros_like(gc_all)], axis=0).T
        glast_all = gc_all[C - 1:C, :]
        dec.append(dict(beta=_sigmoid(sm), gc=gc_all, gc_t=gc_t, eg=jnp.exp(gc_all),
                        ek=jnp.exp(glast_all - gc_all), dl=jnp.exp(glast_all)))

    pairs = [(c, h) for c in chunks for h in heads]
    idx = range(len(pairs))
    q, k, kb, eg, gamma, rhs = [], [], [], [], [], []
    for c, h in pairs:
        lane = DN_HEADS + h
        base = c * C
        qh = load(base, h * DN_DK)
        kh = load(base, DN_W + h * DN_DK)
        vh = load(base, 2 * DN_W + h * DN_DV)
        qh = qh * (lax.rsqrt(jnp.sum(qh * qh, axis=-1, keepdims=True) + NORM_EPS) * (DN_DK ** -0.5))
        kh = kh * lax.rsqrt(jnp.sum(kh * kh, axis=-1, keepdims=True) + NORM_EPS)
        beta = bcast(dec[c]["beta"], h)
        egh = bcast(dec[c]["eg"], lane)
        kbh = kh * beta
        q.append(qh)
        k.append(kh)
        kb.append(kbh)
        eg.append(egh)
        rhs.append(jnp.concatenate([vh * beta, kbh * egh], axis=1))
        diff = bcast(dec[c]["gc"], lane)[:, 0:C] - dec[c]["gc_t"][lane:lane + 1, 0:C]
        gamma.append(jnp.exp(jnp.where(causal, diff, -jnp.inf)))

    a = [_mm_nt(jnp.concatenate([kb[i], q[i]], axis=0), k[i]) for i in idx]
    p = [jnp.where(strict, -(a[i][0:C] * gamma[i]), 0.0) for i in idx]
    aqk = [a[i][C:] * gamma[i] for i in idx]
    t = [eye + p[i] for i in idx]
    for _ in range(5):
        p = [_mm(p[i], p[i]) for i in idx]
        t = [t[i] + _mm(p[i], t[i]) for i in idx]
    x = [_mm(t[i], rhs[i]) for i in idx]

    s = [st_ref[h] for h in heads]
    for c in chunks:
        base = c * C
        at = lambda lst, h: lst[c * DN_HEADS + h]
        ws_qs = [_mm(jnp.concatenate([at(x, h)[:, DN_DV:], at(q, h) * at(eg, h)], axis=0), s[h]) for h in heads]
        v_new = [at(x, h)[:, 0:DN_DV] - ws_qs[h][0:C] for h in heads]
        o = [ws_qs[h][C:] + _mm(at(aqk, h), v_new[h]) for h in heads]
        s_next = []
        for h in heads:
            lane = DN_HEADS + h
            kd = at(k, h) * bcast(dec[c]["ek"], lane)
            dl = jnp.broadcast_to(dec[c]["dl"][:, lane:lane + 1], (DN_DK, DN_DV))
            s_next.append(s[h] * dl + _mm_tn(kd, v_new[h]))
        s = s_next
        for h in heads:
            y = o[h] * lax.rsqrt(jnp.mean(o[h] * o[h], axis=-1, keepdims=True) + NORM_EPS) * nw
            gate = z_ref[base:base + C, h * DN_DV:(h + 1) * DN_DV].astype(F32)
            o_ref[base:base + C, h * DN_DV:(h + 1) * DN_DV] = (y * gate).astype(o_ref.dtype)
    for h in heads:
        st_ref[h] = s[h]


def _deltanet_call(qkv, small, p2, a_log, dt_bias, dn_norm_w, B, S):
    tb = DN_TB
    nt = S // tb
    hp = jnp.zeros((8, LANES), F32)
    hp = hp.at[0, DN_HEADS:2 * DN_HEADS].set(a_log.astype(F32)).at[1, DN_HEADS:2 * DN_HEADS].set(dt_bias.astype(F32))
    nw = dn_norm_w.astype(F32).reshape(1, DN_DV)
    return pl.pallas_call(
        _deltanet_kernel,
        grid=(B, nt),
        in_specs=[pl.BlockSpec((tb, 3 * DN_W), lambda b, i: (b * nt + i, 0)),
                  pl.BlockSpec((tb, LANES), lambda b, i: (b * nt + i, 0)),
                  pl.BlockSpec((tb, DN_W), lambda b, i: (b * nt + i, 0)),
                  pl.BlockSpec((8, LANES), lambda b, i: (0, 0)),
                  pl.BlockSpec((1, DN_DV), lambda b, i: (0, 0))],
        out_specs=pl.BlockSpec((tb, DN_W), lambda b, i: (b * nt + i, 0)),
        out_shape=jax.ShapeDtypeStruct((B * S, DN_W), BF16),
        scratch_shapes=[pltpu.VMEM((DN_HEADS, DN_DK, DN_DV), F32)],
        compiler_params=pltpu.CompilerParams(dimension_semantics=("arbitrary", "arbitrary"),
                                             vmem_limit_bytes=VMEM_LIMIT),
        name="deltanet",
    )(qkv, small, p2, hp, nw)


def _alibi_slope(idx):
    n = N_DIL * DIL_HEADS
    return 2.0 ** (-8.0 * (idx + 1) / n)


def _attn_kernel(q0_ref, k0_ref, v0_ref, q1_ref, k1_ref, v1_ref, q2_ref, k2_ref, v2_ref, z_ref, o_ref,
                 num_ref, den_ref, mx_ref):
    Q = ATT_BLOCK
    hd = pl.program_id(1)
    span = DIL_GROUPS[0][0] // DIL_GROUPS[0][1]
    dist = (Q + lax.broadcasted_iota(jnp.int32, (Q, 2 * Q), 0) - lax.broadcasted_iota(jnp.int32, (Q, 2 * Q), 1))
    valid = (dist >= 0) & (dist <= span)
    distf = dist.astype(F32)
    scale = DIL_DH ** -0.5

    def slope_of(g):
        s = jnp.float32(_alibi_slope(g * DIL_HEADS))
        for h in range(1, DIL_HEADS):
            s = jnp.where(hd == h, jnp.float32(_alibi_slope(g * DIL_HEADS + h)), s)
        return s

    def blocks(items):
        s = [_mm_nt(q, k) * scale + bias for q, k, v, bias in items]
        mx = [jnp.max(si, axis=-1, keepdims=True) for si in s]
        p = [jnp.exp(si - mi) for si, mi in zip(s, mx)]
        den = [jnp.sum(pi, axis=-1, keepdims=True) for pi in p]
        num = [_mm(pi, it[2]) for pi, it in zip(p, items)]
        return [(n, jnp.broadcast_to(dn, (Q, LANES)), jnp.broadcast_to(m, (Q, LANES)))
                for n, dn, m in zip(num, den, mx)]

    def merge(a, b):
        (n1, d1, m1), (n2, d2, m2) = a, b
        m = jnp.maximum(m1, m2)
        a1 = jnp.exp(m1 - m)
        a2 = jnp.exp(m2 - m)
        return n1 * a1 + n2 * a2, d1 * a1 + d2 * a2, m

    def run_group(g, q_ref, k_ref, v_ref, emit, r_unroll, j_unroll):
        d = DIL_GROUPS[g][1]
        bias2 = jnp.where(valid, -(slope_of(g) * float(d)) * distf, -jnp.inf)
        bias1 = bias2[:, Q:]
        nb = q_ref.shape[-2] // Q
        j_unroll = min(j_unroll, nb)

        def refs(r):
            if d == 1:
                return q_ref, k_ref, v_ref
            return q_ref.at[r], k_ref.at[r], v_ref.at[r]

        def item(r, j):
            qr, kr, vr = refs(r)
            if isinstance(j, int) and j == 0:
                return qr[0:Q, :], kr[0:Q, :], vr[0:Q, :], bias1
            if isinstance(j, int):
                lo, cur = (j - 1) * Q, j * Q
            else:
                lo, cur = pl.multiple_of((j - 1) * Q, Q), pl.multiple_of(j * Q, Q)
            return qr[pl.ds(cur, Q), :], kr[pl.ds(lo, 2 * Q), :], vr[pl.ds(lo, 2 * Q), :], bias2

        def batch(rs, js):
            ids = [(r, j) for r in rs for j in js]
            for (r, j), res in zip(ids, blocks([item(r, j) for r, j in ids])):
                emit(r, j, res)

        def sub(rb, carry):
            rs = [rb * r_unroll + t for t in range(r_unroll)] if d > 1 else [0]
            batch(rs, list(range(j_unroll)))

            def jbatch(jb, c2):
                batch(rs, [jb * j_unroll + t for t in range(j_unroll)])
                return c2

            lax.fori_loop(1, nb // j_unroll, jbatch, 0)
            return carry

        if d == 1:
            sub(0, 0)
        else:
            lax.fori_loop(0, d // r_unroll, sub, 0)

    def rows(g, r, j):
        d = DIL_GROUPS[g][1]
        if d == 1:
            return pl.ds(j * Q, Q) if isinstance(j, int) else pl.ds(pl.multiple_of(j * Q, Q), Q)
        return pl.ds(j * (Q * d) + r, Q, stride=d)

    def emit_first(r, j, res):
        idx = rows(2, r, j)
        num_ref[idx, :] = res[0]
        den_ref[idx, :] = res[1]
        mx_ref[idx, :] = res[2]

    def emit_second(r, j, res):
        idx = rows(1, r, j)
        n, dn, m = merge((num_ref[idx, :], den_ref[idx, :], mx_ref[idx, :]), res)
        num_ref[idx, :] = n
        den_ref[idx, :] = dn
        mx_ref[idx, :] = m

    def emit_last(r, j, res):
        idx = rows(0, r, j)
        n, dn, _ = merge((num_ref[idx, :], den_ref[idx, :], mx_ref[idx, :]), res)
        o_ref[idx, :] = ((n / dn) * z_ref[idx, :].astype(F32)).astype(o_ref.dtype)

    run_group(2, q2_ref, k2_ref, v2_ref, emit_first, r_unroll=2, j_unroll=2)
    run_group(1, q1_ref, k1_ref, v1_ref, emit_second, r_unroll=1, j_unroll=4)
    run_group(0, q0_ref, k0_ref, v0_ref, emit_last, r_unroll=1, j_unroll=4)


def _attn_call(p2, off0, zb_off, g1, g2, B, S):
    H = DIL_HEADS
    p2v = p2.reshape(B, S, p2.shape[1])
    g1v = g1.reshape(B, 4, S // 4, GROUP_W)
    g2v = g2.reshape(B, 16, S // 16, GROUP_W)
    b0 = off0 // LANES
    zb0 = zb_off // LANES

    def nat(cb):
        return pl.BlockSpec((None, S, LANES), lambda b, h: (b, 0, cb + h))

    def perm(d, cb):
        return pl.BlockSpec((None, d, S // d, LANES), lambda b, h: (b, 0, 0, cb + h))

    return pl.pallas_call(
        _attn_kernel,
        grid=(B, H),
        in_specs=[nat(b0), nat(b0 + H), nat(b0 + 2 * H),
                  perm(4, 0), perm(4, H), perm(4, 2 * H),
                  perm(16, 0), perm(16, H), perm(16, 2 * H),
                  nat(zb0)],
        out_specs=pl.BlockSpec((None, S, LANES), lambda b, h: (b, 0, h)),
        out_shape=jax.ShapeDtypeStruct((B, S, DIL_W), BF16),
        scratch_shapes=[pltpu.VMEM((S, LANES), F32)] * 3,
        compiler_params=pltpu.CompilerParams(dimension_semantics=("arbitrary", "arbitrary"),
                                             vmem_limit_bytes=VMEM_LIMIT),
        name="dilated_attention",
    )(p2v, p2v, p2v, g1v, g1v, g1v, g2v, g2v, g2v, p2v)


def _out_kernel(oa_ref, ob_ref, ga_ref, gb_ref, x_ref, wa_ref, wb_ref, wo_ref, fw_ref, o_ref, *, final):
    ya = jnp.dot(oa_ref[...], wa_ref[...], preferred_element_type=F32)
    yb = jnp.dot(ob_ref[...], wb_ref[...], preferred_element_type=F32)
    merged = ga_ref[...].astype(F32) * ya + gb_ref[...].astype(F32) * yb
    xn = x_ref[...] + jnp.dot(merged.astype(BF16), wo_ref[...], preferred_element_type=F32)
    if final:
        xn = xn * lax.rsqrt(jnp.mean(xn * xn, axis=-1, keepdims=True) + NORM_EPS) * fw_ref[...]
    o_ref[...] = xn


def _out_call(oa, ob, p2, ga_off, gb_off, x2d, wa, wb, wo, fw, final):
    T, D = x2d.shape
    tm = OUT_TM
    ga_b = ga_off // D
    gb_b = gb_off // D
    const = lambda i: (0, 0)
    return pl.pallas_call(
        functools.partial(_out_kernel, final=final),
        grid=(T // tm,),
        in_specs=[pl.BlockSpec((tm, DN_W), lambda i: (i, 0)),
                  pl.BlockSpec((tm, DIL_W), lambda i: (i, 0)),
                  pl.BlockSpec((tm, D), lambda i: (i, ga_b)),
                  pl.BlockSpec((tm, D), lambda i: (i, gb_b)),
                  pl.BlockSpec((tm, D), lambda i: (i, 0)),
                  pl.BlockSpec((DN_W, D), const),
                  pl.BlockSpec((DIL_W, D), const),
                  pl.BlockSpec((D, D), const),
                  pl.BlockSpec((1, D), const)],
        out_specs=pl.BlockSpec((tm, D), lambda i: (i, 0)),
        out_shape=jax.ShapeDtypeStruct((T, D), F32),
        compiler_params=pltpu.CompilerParams(dimension_semantics=("arbitrary",),
                                             vmem_limit_bytes=VMEM_LIMIT),
        name="out_merge",
    )(oa, ob, p2, p2, x2d, wa, wb, wo, fw)


def _split_cols(w):
    out, start = [], 0
    for s in PROJ_SIZES:
        out.append(w[:, start:start + s])
        start += s
    return out


def _layer(x, norm_w, w_in, conv_w, a_log, dt_bias, dn_norm_w, w_o_dn, w_o_dil, w_out, final_w, final):
    B, S, D = x.shape
    T = B * S
    (wq_a, wk_a, wv_a, wz_a, wb_a, wa_a, wq_b, wk_b, wv_b, wz_b, wg_a, wg_b) = _split_cols(w_in)
    w_small = jnp.concatenate([wb_a, wa_a, jnp.zeros((D, LANES - 2 * DN_HEADS), w_in.dtype)], axis=1).astype(BF16)
    w1 = jnp.concatenate([wq_a, wk_a, wv_a], axis=1).astype(BF16)

    def group_w(g):
        sl = slice(g * DIL_W, (g + 1) * DIL_W)
        return jnp.concatenate([wq_b[:, sl], wk_b[:, sl], wv_b[:, sl]], axis=1)

    w2 = jnp.concatenate([wz_a, wz_b, group_w(0), wg_a, wg_b], axis=1).astype(BF16)
    zb_off = DN_W
    off0 = zb_off + DIL_W
    ga_off = off0 + GROUP_W
    gb_off = ga_off + D
    tn = PROJ_TN
    segs2 = ((0, off0 // tn, "silu"), (off0 // tn, ga_off // tn, "none"), (ga_off // tn, (gb_off + D) // tn, "sigmoid"))

    h, h4, h16, small = _norm_call(x, norm_w.astype(F32), w_small)
    qkv = _proj_conv_call(h.reshape(T, D), w1, conv_w, S, "proj_conv_deltanet")
    p2 = _proj_call(h.reshape(T, D), w2, segs2, BF16, "proj_gates_group0")
    g1 = _proj_call(h4.reshape(T, D), group_w(1).astype(BF16), ((0, GROUP_W // tn, "none"),), BF16, "proj_group1")
    g2 = _proj_call(h16.reshape(T, D), group_w(2).astype(BF16), ((0, GROUP_W // tn, "none"),), BF16, "proj_group2")

    o_a = _deltanet_call(qkv, small.reshape(T, LANES), p2, a_log, dt_bias, dn_norm_w, B, S)
    o_b = _attn_call(p2, off0, zb_off, g1, g2, B, S)
    out = _out_call(o_a, o_b.reshape(T, DIL_W), p2, ga_off, gb_off, x.reshape(T, D),
                    w_o_dn.astype(BF16), w_o_dil.astype(BF16), w_out.astype(BF16),
                    final_w.astype(F32).reshape(1, D), final)
    return out.reshape(B, S, D)


def kernel(x, norm_w, w_in, conv_w, a_log, dt_bias, dn_norm_w, w_o_dn, w_o_dil, w_out, final_norm_w):
    depth = norm_w.shape[0]
    for l in range(depth):
        x = _layer(x, norm_w[l], w_in[l], conv_w[l], a_log[l], dt_bias[l], dn_norm_w[l],
                   w_o_dn[l], w_o_dil[l], w_out[l], final_norm_w, final=(l == depth - 1))
    return x
```

```python
import functools
import math

import jax
import jax.numpy as jnp
import numpy as np
from jax import lax
from jax.experimental import pallas as pl
from jax.experimental.pallas import tpu as pltpu

F32 = jnp.float32
BF16 = jnp.bfloat16

D_MODEL = 1024
DN_HEADS = 8
DN_DK = 128
DN_DV = 128
DN_CONV = 4
DN_CHUNK = 64
DIL_GROUPS = ((128, 1), (512, 4), (2048, 16))
DIL_HEADS = 4
DIL_DH = 128
ATT_BLOCK = 128
NORM_EPS = 1e-6
N_DIL = len(DIL_GROUPS)
DN_W = DN_HEADS * DN_DK
DIL_W = DIL_HEADS * DIL_DH
GROUP_W = 3 * DIL_W
PROJ_SIZES = (DN_W, DN_W, DN_W, DN_W, DN_HEADS, DN_HEADS,
              N_DIL * DIL_W, N_DIL * DIL_W, N_DIL * DIL_W, DIL_W, D_MODEL, D_MODEL)

LANES = 128
VMEM_LIMIT = 56 * 1024 * 1024

NORM_TM = 512
PROJ_TM = 2048
PROJ_TN = 512
PROJ_SUB = 256
DN_TB = 512
DN_UNIT = 2
OUT_TM = 1024


def _sigmoid(x):
    return 1.0 / (1.0 + jnp.exp(-x))


def _silu(x):
    return x * _sigmoid(x)


def _softplus(x):
    return jnp.maximum(x, 0.0) + jnp.log1p(jnp.exp(-jnp.abs(x)))


def _mm(a, b):
    return lax.dot_general(a.astype(BF16), b.astype(BF16), (((1,), (0,)), ((), ())),
                           preferred_element_type=F32)


def _mm_nt(a, b):
    return lax.dot_general(a.astype(BF16), b.astype(BF16), (((1,), (1,)), ((), ())),
                           preferred_element_type=F32)


def _mm_tn(a, b):
    return lax.dot_general(a.astype(BF16), b.astype(BF16), (((0,), (0,)), ((), ())),
                           preferred_element_type=F32)


def _norm_kernel(x_ref, w_ref, ws_ref, h_ref, h4_ref, h16_ref, sm_ref, slab_ref):
    x = x_ref[...]
    y = x * lax.rsqrt(jnp.mean(x * x, axis=-1, keepdims=True) + NORM_EPS) * w_ref[...]
    hb = y.astype(BF16)
    h_ref[...] = hb
    sm_ref[...] = jnp.dot(hb, ws_ref[...], preferred_element_type=F32)
    n_slabs = D_MODEL // LANES
    for c in range(n_slabs):
        slab_ref[c] = y[:, c * LANES:(c + 1) * LANES]
    tm = x.shape[0]
    for d, out in ((4, h4_ref), (16, h16_ref)):
        n = tm // d
        for r in range(d):
            for c in range(n_slabs):
                out[r, :, c * LANES:(c + 1) * LANES] = slab_ref[c, pl.ds(r, n, stride=d), :].astype(BF16)


def _norm_call(x, w, w_small):
    B, S, D = x.shape
    tm = NORM_TM
    nt = S // tm
    return pl.pallas_call(
        _norm_kernel,
        grid=(B, nt),
        in_specs=[pl.BlockSpec((None, tm, D), lambda b, i: (b, i, 0)),
                  pl.BlockSpec((1, D), lambda b, i: (0, 0)),
                  pl.BlockSpec((D, LANES), lambda b, i: (0, 0))],
        out_specs=[pl.BlockSpec((None, tm, D), lambda b, i: (b, i, 0)),
                   pl.BlockSpec((None, 4, tm // 4, D), lambda b, i: (b, 0, i, 0)),
                   pl.BlockSpec((None, 16, tm // 16, D), lambda b, i: (b, 0, i, 0)),
                   pl.BlockSpec((None, tm, LANES), lambda b, i: (b, i, 0))],
        out_shape=[jax.ShapeDtypeStruct((B, S, D), BF16),
                   jax.ShapeDtypeStruct((B, 4, S // 4, D), BF16),
                   jax.ShapeDtypeStruct((B, 16, S // 16, D), BF16),
                   jax.ShapeDtypeStruct((B, S, LANES), F32)],
        scratch_shapes=[pltpu.VMEM((D // LANES, tm, LANES), F32)],
        compiler_params=pltpu.CompilerParams(dimension_semantics=("arbitrary", "arbitrary"),
                                             vmem_limit_bytes=VMEM_LIMIT),
        name="norm_deinterleave",
    )(x, w.reshape(1, D), w_small)


_ACTS = {"none": lambda v: v, "silu": _silu, "sigmoid": _sigmoid}


def _proj_kernel(h_ref, w_ref, o_ref, *, segs):
    acc = jnp.dot(h_ref[...], w_ref[...], preferred_element_type=F32)
    if len(segs) == 1:
        o_ref[...] = _ACTS[segs[0][2]](acc).astype(o_ref.dtype)
        return
    j = pl.program_id(1)
    for lo, hi, act in segs:
        @pl.when((j >= lo) & (j < hi))
        def _(act=act):
            o_ref[...] = _ACTS[act](acc).astype(o_ref.dtype)


def _proj_call(h2d, w, segs, out_dtype, name):
    T, D = h2d.shape
    N = w.shape[1]
    tm, tn = PROJ_TM, PROJ_TN
    return pl.pallas_call(
        functools.partial(_proj_kernel, segs=segs),
        grid=(T // tm, N // tn),
        in_specs=[pl.BlockSpec((tm, D), lambda i, j: (i, 0)),
                  pl.BlockSpec((D, tn), lambda i, j: (0, j))],
        out_specs=pl.BlockSpec((tm, tn), lambda i, j: (i, j)),
        out_shape=jax.ShapeDtypeStruct((T, N), out_dtype),
        compiler_params=pltpu.CompilerParams(dimension_semantics=("arbitrary", "arbitrary"),
                                             vmem_limit_bytes=VMEM_LIMIT),
        name=name,
    )(h2d, w)


def _proj_conv_kernel(h_ref, w_ref, cw_ref, o_ref, tail_ref, *, tiles_per_seq):
    i = pl.program_id(1)
    tm = h_ref.shape[0]
    sub = PROJ_SUB

    @pl.when(i % tiles_per_seq == 0)
    def _():
        tail_ref[...] = jnp.zeros_like(tail_ref)

    prev = tail_ref[...]
    taps = [cw_ref[k:k + 1, :] for k in range(DN_CONV)]
    for m in range(tm // sub):
        acc = jnp.dot(h_ref[m * sub:(m + 1) * sub, :], w_ref[...], preferred_element_type=F32)
        ext = jnp.concatenate([prev, acc], axis=0)
        y = taps[DN_CONV - 1] * acc
        for k in range(DN_CONV - 1):
            y = y + taps[k] * pltpu.roll(ext, DN_CONV - 1 - k, axis=0)[8:, :]
        o_ref[m * sub:(m + 1) * sub, :] = _silu(y).astype(o_ref.dtype)
        prev = acc[sub - 8:sub, :]
    tail_ref[...] = prev


def _proj_conv_call(h2d, w, conv_w, S, name):
    T, D = h2d.shape
    N = w.shape[1]
    tm, tn = PROJ_TM, PROJ_TN
    cw = jnp.zeros((8, N), F32).at[0:DN_CONV].set(conv_w.astype(F32))
    return pl.pallas_call(
        functools.partial(_proj_conv_kernel, tiles_per_seq=S // tm),
        grid=(N // tn, T // tm),
        in_specs=[pl.BlockSpec((tm, D), lambda j, i: (i, 0)),
                  pl.BlockSpec((D, tn), lambda j, i: (0, j)),
                  pl.BlockSpec((8, tn), lambda j, i: (0, j))],
        out_specs=pl.BlockSpec((tm, tn), lambda j, i: (i, j)),
        out_shape=jax.ShapeDtypeStruct((T, N), BF16),
        scratch_shapes=[pltpu.VMEM((8, tn), F32)],
        compiler_params=pltpu.CompilerParams(dimension_semantics=("arbitrary", "arbitrary"),
                                             vmem_limit_bytes=VMEM_LIMIT),
        name=name,
    )(h2d, w, cw)


def _deltanet_kernel(qkv_ref, sm_ref, z_ref, hp_ref, nw_ref, o_ref, st_ref):
    C = DN_CHUNK
    tb = o_ref.shape[0]

    @pl.when(pl.program_id(1) == 0)
    def _():
        st_ref[...] = jnp.zeros_like(st_ref)

    row = lax.broadcasted_iota(jnp.int32, (C, C), 0)
    col = lax.broadcasted_iota(jnp.int32, (C, C), 1)
    causal = row >= col
    strict = row > col
    eye =(row == col).astype(F32)
    hp = hp_ref[...]
    neg_a = -jnp.exp(hp[0:1, :])
    dt_b = hp[1:2, :]
    nw = nw_ref[...]

    heads = range(DN_HEADS)

    def bcast(a, lane):
        return jnp.broadcast_to(a[:, lane:lane + 1], (C, LANES))

    def load(base, c0):
        return qkv_ref[base:base + C, c0:c0 + LANES].astype(F32)

    row_l = lax.broadcasted_iota(jnp.int32, (C, LANES), 0)

    def cumsum_rows(v):
        shift = 1
        while shift < C:
            v = v + jnp.where(row_l >= shift, pltpu.roll(v, shift, axis=0), 0.0)
            shift *= 2
        return v

    def prologue(unit, out):
        dec = {}
        for c in unit:
            sm = sm_ref[c * C:(c + 1) * C, :]
            g_all = neg_a * _softplus(sm + dt_b)
            gc_all = cumsum_rows(g_all)
            gc_t = jnp.concatenate([gc_all, jnp.zeros_like(gc_all)], axis=0).T
            glast_all = gc_all[C - 1:C, :]
            dec[c] = dict(beta=_sigmoid(sm), gc=gc_all, gc_t=gc_t, eg=jnp.exp(gc_all),
                          ek=jnp.exp(glast_all - gc_all), dl=jnp.exp(glast_all))
            yield
        q, k, kb, eg, gamma, rhs = [], [], [], [], [], []
        for c in unit:
            for h in heads:
                lane = DN_HEADS + h
                base = c * C
                qh = load(base, h * DN_DK)
                kh = load(base, DN_W + h * DN_DK)
                vh = load(base, 2 * DN_W + h * DN_DV)
                qh = qh * (lax.rsqrt(jnp.sum(qh * qh, axis=-1, keepdims=True) + NORM_EPS) * (DN_DK ** -0.5))
                kh = kh * lax.rsqrt(jnp.sum(kh * kh, axis=-1, keepdims=True) + NORM_EPS)
                beta = bcast(dec[c]["beta"], h)
                egh = bcast(dec[c]["eg"], lane)
                kbh = kh * beta
                q.append(qh)
                k.append(kh)
                kb.append(kbh)
                eg.append(egh)
                rhs.append(jnp.concatenate([vh * beta, kbh * egh], axis=1))
                diff = bcast(dec[c]["gc"], lane)[:, 0:C] - dec[c]["gc_t"][lane:lane + 1, 0:C]
                gamma.append(jnp.exp(jnp.where(causal, diff, -jnp.inf)))
                if h % 2 == 1:
                    yield
        out.update(dec=dec, q=q, k=k, kb=kb, eg=eg, gamma=gamma, rhs=rhs)

    def solve(unit, pro, out):
        idx = range(len(unit) * DN_HEADS)
        q, k, kb, gamma, rhs = pro["q"], pro["k"], pro["kb"], pro["gamma"], pro["rhs"]
        a = [_mm_nt(jnp.concatenate([kb[i], q[i]], axis=0), k[i]) for i in idx]
        yield
        p = [jnp.where(strict, -(a[i][0:C] * gamma[i]), 0.0) for i in idx]
        aqk = [a[i][C:] * gamma[i] for i in idx]
        t = [eye + p[i] for i in idx]
        p = [_mm(p[i], p[i]) for i in idx]
        yield
        for _ in range(4):
            r = [_mm(jnp.concatenate([p[i], t[i]], axis=0), p[i]) for i in idx]
            t = [t[i] + r[i][C:] for i in idx]
            p = [r[i][0:C] for i in idx]
            yield
        t = [t[i] + _mm(t[i], p[i]) for i in idx]
        yield
        out.update(x=[_mm(t[i], rhs[i]) for i in idx], aqk=aqk)

    state = [st_ref[h] for h in heads]

    def recur(unit, pro, sol):
        for n, c in enumerate(unit):
            base = c * C
            at = lambda lst, h: lst[n * DN_HEADS + h]
            ws_qs = [_mm(jnp.concatenate([at(sol["x"], h)[:, DN_DV:], at(pro["q"], h) * at(pro["eg"], h)], axis=0),
                         state[h]) for h in heads]
            yield
            v_new = [at(sol["x"], h)[:, 0:DN_DV] - ws_qs[h][0:C] for h in heads]
            o = [ws_qs[h][C:] + _mm(at(sol["aqk"], h), v_new[h]) for h in heads]
            for h in heads:
                lane = DN_HEADS + h
                kd = at(pro["k"], h) * bcast(pro["dec"][c]["ek"], lane)
                dl = jnp.broadcast_to(pro["dec"][c]["dl"][:, lane:lane + 1], (DN_DK, DN_DV))
                state[h] = state[h] * dl + _mm_tn(kd, v_new[h])
            yield
            for h in heads:
                y = o[h] * lax.rsqrt(jnp.mean(o[h] * o[h], axis=-1, keepdims=True) + NORM_EPS) * nw
                gate = z_ref[base:base + C, h * DN_DV:(h + 1) * DN_DV].astype(F32)
                o_ref[base:base + C, h * DN_DV:(h + 1) * DN_DV] = (y * gate).astype(o_ref.dtype)
                if h % 4 == 3:
                    yield

    def interleave(gens):
        gens = list(gens)
        while gens:
            for g in list(gens):
                try:
                    next(g)
                except StopIteration:
                    gens.remove(g)

    n_chunks = tb // C
    units = [list(range(u, min(u + DN_UNIT, n_chunks))) for u in range(0, n_chunks, DN_UNIT)]
    pro = [dict() for _ in units]
    sol = [dict() for _ in units]
    for tick in range(len(units) + 2):
        active = []
        if 0 <= tick - 2 < len(units):
            active.append(recur(units[tick - 2], pro[tick - 2], sol[tick - 2]))
        if 0 <= tick - 1 < len(units):
            active.append(solve(units[tick - 1], pro[tick - 1], sol[tick - 1]))
        if tick < len(units):
            active.append(prologue(units[tick], pro[tick]))
        interleave(active)
    for h in heads:
        st_ref[h] = state[h]


def _deltanet_call(qkv, small, p2, a_log, dt_bias, dn_norm_w, B, S):
    tb = DN_TB
    nt = S // tb
    hp = jnp.zeros((8, LANES), F32)
    hp = hp.at[0, DN_HEADS:2 * DN_HEADS].set(a_log.astype(F32)).at[1, DN_HEADS:2 * DN_HEADS].set(dt_bias.astype(F32))
    nw = dn_norm_w.astype(F32).reshape(1, DN_DV)
    return pl.pallas_call(
        _deltanet_kernel,
        grid=(B, nt),
        in_specs=[pl.BlockSpec((tb, 3 * DN_W), lambda b, i: (b * nt + i, 0)),
                  pl.BlockSpec((tb, LANES), lambda b, i: (b * nt + i, 0)),
                  pl.BlockSpec((tb, DN_W), lambda b, i: (b * nt + i, 0)),
                  pl.BlockSpec((8, LANES), lambda b, i: (0, 0)),
                  pl.BlockSpec((1, DN_DV), lambda b, i: (0, 0))],
        out_specs=pl.BlockSpec((tb, DN_W), lambda b, i: (b * nt + i, 0)),
        out_shape=jax.ShapeDtypeStruct((B * S, DN_W), BF16),
        scratch_shapes=[pltpu.VMEM((DN_HEADS, DN_DK, DN_DV), F32)],
        compiler_params=pltpu.CompilerParams(dimension_semantics=("arbitrary", "arbitrary"),
                                             vmem_limit_bytes=VMEM_LIMIT),
        name="deltanet",
    )(qkv, small, p2, hp, nw)


def _alibi_slope(idx):
    n = N_DIL * DIL_HEADS
    return 2.0 ** (-8.0 * (idx + 1) / n)


def _attn_kernel(q0_ref, k0_ref, v0_ref, q1_ref, k1_ref, v1_ref, q2_ref, k2_ref, v2_ref, z_ref, o_ref,
                 num_ref, den_ref, mx_ref):
    Q = ATT_BLOCK
    hd = pl.program_id(1)
    span = DIL_GROUPS[0][0] // DIL_GROUPS[0][1]
    dist = (Q + lax.broadcasted_iota(jnp.int32, (Q, 2 * Q), 0) - lax.broadcasted_iota(jnp.int32, (Q, 2 * Q), 1))
    valid = (dist >= 0) & (dist <= span)
    distf = dist.astype(F32)
    scale = DIL_DH ** -0.5

    def slope_of(g):
        s = jnp.float32(_alibi_slope(g * DIL_HEADS))
        for h in range(1, DIL_HEADS):
            s = jnp.where(hd == h, jnp.float32(_alibi_slope(g * DIL_HEADS + h)), s)
        return s

    def blocks(items):
        s = [_mm_nt(q, k) * scale + bias for q, k, v, bias in items]
        mx = [jnp.max(si, axis=-1, keepdims=True) for si in s]
        p = [jnp.exp(si - mi) for si, mi in zip(s, mx)]
        den = [jnp.sum(pi, axis=-1, keepdims=True) for pi in p]
        num = [_mm(pi, it[2]) for pi, it in zip(p, items)]
        return [(n, jnp.broadcast_to(dn, (Q, LANES)), jnp.broadcast_to(m, (Q, LANES)))
                for n, dn, m in zip(num, den, mx)]

    def merge(a, b):
        (n1, d1, m1), (n2, d2, m2) = a, b
        m = jnp.maximum(m1, m2)
        a1 = jnp.exp(m1 - m)
        a2 = jnp.exp(m2 - m)
        return n1 * a1 + n2 * a2, d1 * a1 + d2 * a2, m

    def run_group(g, q_ref, k_ref, v_ref, emit, r_unroll, j_unroll):
        d = DIL_GROUPS[g][1]
        bias2 = jnp.where(valid, -(slope_of(g) * float(d)) * distf, -jnp.inf)
        bias1 = bias2[:, Q:]
        nb = q_ref.shape[-2] // Q
        j_unroll = min(j_unroll, nb)

        def refs(r):
            if d == 1:
                return q_ref, k_ref, v_ref
            return q_ref.at[r], k_ref.at[r], v_ref.at[r]

        def item(r, j):
            qr, kr, vr = refs(r)
            if isinstance(j, int) and j == 0:
                return qr[0:Q, :], kr[0:Q, :], vr[0:Q, :], bias1
            if isinstance(j, int):
                lo, cur = (j - 1) * Q, j * Q
            else:
                lo, cur = pl.multiple_of((j - 1) * Q, Q), pl.multiple_of(j * Q, Q)
            return qr[pl.ds(cur, Q), :], kr[pl.ds(lo, 2 * Q), :], vr[pl.ds(lo, 2 * Q), :], bias2

        def batch(rs, js):
            ids = [(r, j) for r in rs for j in js]
            for (r, j), res in zip(ids, blocks([item(r, j) for r, j in ids])):
                emit(r, j, res)

        def sub(rb, carry):
            rs = [rb * r_unroll + t for t in range(r_unroll)] if d > 1 else [0]
            batch(rs, list(range(j_unroll)))

            def jbatch(jb, c2):
                batch(rs, [jb * j_unroll + t for t in range(j_unroll)])
                return c2

            lax.fori_loop(1, nb // j_unroll, jbatch, 0)
            return carry

        if d == 1:
            sub(0, 0)
        else:
            lax.fori_loop(0, d // r_unroll, sub, 0)

    def rows(g, r, j):
        d = DIL_GROUPS[g][1]
        if d == 1:
            return pl.ds(j * Q, Q) if isinstance(j, int) else pl.ds(pl.multiple_of(j * Q, Q), Q)
        return pl.ds(j * (Q * d) + r, Q, stride=d)

    def emit_first(r, j, res):
        idx = rows(2, r, j)
        num_ref[idx, :] = res[0]
        den_ref[idx, :] = res[1]
        mx_ref[idx, :] = res[2]

    def emit_second(r, j, res):
        idx = rows(1, r, j)
        n, dn, m = merge((num_ref[idx, :], den_ref[idx, :], mx_ref[idx, :]), res)
        num_ref[idx, :] = n
        den_ref[idx, :] = dn
        mx_ref[idx, :] = m

    def emit_last(r, j, res):
        idx = rows(0, r, j)
        n, dn, _ = merge((num_ref[idx, :], den_ref[idx, :], mx_ref[idx, :]), res)
        o_ref[idx, :] = ((n / dn) * z_ref[idx, :].astype(F32)).astype(o_ref.dtype)

    run_group(2, q2_ref, k2_ref, v2_ref, emit_first, r_unroll=4, j_unroll=2)
    run_group(1, q1_ref, k1_ref, v1_ref, emit_second, r_unroll=1, j_unroll=8)
    run_group(0, q0_ref, k0_ref, v0_ref, emit_last, r_unroll=1, j_unroll=8)


def _attn_call(p2, off0, zb_off, g1, g2, B, S):
    H = DIL_HEADS
    p2v = p2.reshape(B, S, p2.shape[1])
    g1v = g1.reshape(B, 4, S // 4, GROUP_W)
    g2v = g2.reshape(B, 16, S // 16, GROUP_W)
    b0 = off0 // LANES
    zb0 = zb_off // LANES

    def nat(cb):
        return pl.BlockSpec((None, S, LANES), lambda b, h: (b, 0, cb + h))

    def perm(d, cb):
        return pl.BlockSpec((None, d, S // d, LANES), lambda b, h: (b, 0, 0, cb + h))

    return pl.pallas_call(
        _attn_kernel,
        grid=(B, H),
        in_specs=[nat(b0), nat(b0 + H), nat(b0 + 2 * H),
                  perm(4, 0), perm(4, H), perm(4, 2 * H),
                  perm(16, 0), perm(16, H), perm(16, 2 * H),
                  nat(zb0)],
        out_specs=pl.BlockSpec((None, S, LANES), lambda b, h: (b, 0, h)),
        out_shape=jax.ShapeDtypeStruct((B, S, DIL_W), BF16),
        scratch_shapes=[pltpu.VMEM((S, LANES), F32)] * 3,
        compiler_params=pltpu.CompilerParams(dimension_semantics=("arbitrary", "arbitrary"),
                                             vmem_limit_bytes=VMEM_LIMIT),
        name="dilated_attention",
    )(p2v, p2v, p2v, g1v, g1v, g1v, g2v, g2v, g2v, p2v)


def _out_kernel(oa_ref, ob_ref, ga_ref, gb_ref, x_ref, wa_ref, wb_ref, wo_ref, fw_ref, o_ref, *, final):
    ya = jnp.dot(oa_ref[...], wa_ref[...], preferred_element_type=F32)
    yb = jnp.dot(ob_ref[...], wb_ref[...], preferred_element_type=F32)
    merged = ga_ref[...].astype(F32) * ya + gb_ref[...].astype(F32) * yb
    xn = x_ref[...] + jnp.dot(merged.astype(BF16), wo_ref[...], preferred_element_type=F32)
    if final:
        xn = xn * lax.rsqrt(jnp.mean(xn * xn, axis=-1, keepdims=True) + NORM_EPS) * fw_ref[...]
    o_ref[...] = xn


def _out_call(oa, ob, p2, ga_off, gb_off, x2d, wa, wb, wo, fw, final):
    T, D = x2d.shape
    tm = OUT_TM
    ga_b = ga_off // D
    gb_b = gb_off // D
    const = lambda i: (0, 0)
    return pl.pallas_call(
        functools.partial(_out_kernel, final=final),
        grid=(T // tm,),
        in_specs=[pl.BlockSpec((tm, DN_W), lambda i: (i, 0)),
                  pl.BlockSpec((tm, DIL_W), lambda i: (i, 0)),
                  pl.BlockSpec((tm, D), lambda i: (i, ga_b)),
                  pl.BlockSpec((tm, D), lambda i: (i, gb_b)),
                  pl.BlockSpec((tm, D), lambda i: (i, 0)),
                  pl.BlockSpec((DN_W, D), const),
                  pl.BlockSpec((DIL_W, D), const),
                  pl.BlockSpec((D, D), const),
                  pl.BlockSpec((1, D), const)],
        out_specs=pl.BlockSpec((tm, D), lambda i: (i, 0)),
        out_shape=jax.ShapeDtypeStruct((T, D), F32),
        compiler_params=pltpu.CompilerParams(dimension_semantics=("arbitrary",),
                                             vmem_limit_bytes=VMEM_LIMIT),
        name="out_merge",
    )(oa, ob, p2, p2, x2d, wa, wb, wo, fw)


def _split_cols(w):
    out, start = [], 0
    for s in PROJ_SIZES:
        out.append(w[:, start:start + s])
        start += s
    return out


def _layer(x, norm_w, w_in, conv_w, a_log, dt_bias, dn_norm_w, w_o_dn, w_o_dil, w_out, final_w, final):
    B, S, D = x.shape
    T = B * S
    (wq_a, wk_a, wv_a, wz_a, wb_a, wa_a, wq_b, wk_b, wv_b, wz_b, wg_a, wg_b) = _split_cols(w_in)
    w_small = jnp.concatenate([wb_a, wa_a, jnp.zeros((D, LANES - 2 * DN_HEADS), w_in.dtype)], axis=1).astype(BF16)
    w1 = jnp.concatenate([wq_a, wk_a, wv_a], axis=1).astype(BF16)

    def group_w(g):
        sl = slice(g * DIL_W, (g + 1) * DIL_W)
        return jnp.concatenate([wq_b[:, sl], wk_b[:, sl], wv_b[:, sl]], axis=1)

    w2 = jnp.concatenate([wz_a, wz_b, group_w(0), wg_a, wg_b], axis=1).astype(BF16)
    zb_off = DN_W
    off0 = zb_off + DIL_W
    ga_off = off0 + GROUP_W
    gb_off = ga_off + D
    tn = PROJ_TN
    segs2 = ((0, off0 // tn, "silu"), (off0 // tn, ga_off // tn, "none"), (ga_off // tn, (gb_off + D) // tn, "sigmoid"))

    h, h4, h16, small = _norm_call(x, norm_w.astype(F32), w_small)
    qkv = _proj_conv_call(h.reshape(T, D), w1, conv_w, S, "proj_conv_deltanet")
    p2 = _proj_call(h.reshape(T, D), w2, segs2, BF16, "proj_gates_group0")
    g1 = _proj_call(h4.reshape(T, D), group_w(1).astype(BF16), ((0, GROUP_W // tn, "none"),), BF16, "proj_group1")
    g2 = _proj_call(h16.reshape(T, D), group_w(2).astype(BF16), ((0, GROUP_W // tn, "none"),), BF16, "proj_group2")

    o_a = _deltanet_call(qkv, small.reshape(T, LANES), p2, a_log, dt_bias, dn_norm_w, B, S)
    o_b = _attn_call(p2, off0, zb_off, g1, g2, B, S)
    out = _out_call(o_a, o_b.reshape(T, DIL_W), p2, ga_off, gb_off, x.reshape(T, D),
                    w_o_dn.astype(BF16), w_o_dil.astype(BF16), w_out.astype(BF16),
                    final_w.astype(F32).reshape(1, D), final)
    return out.reshape(B, S, D)


def kernel(x, norm_w, w_in, conv_w, a_log, dt_bias, dn_norm_w, w_o_dn, w_o_dil, w_out, final_norm_w):
    depth = norm_w.shape[0]
    for l in range(depth):
        x = _layer(x, norm_w[l], w_in[l], conv_w[l], a_log[l], dt_bias[l], dn_norm_w[l],
                   w_o_dn[l], w_o_dil[l], w_out[l], final_norm_w, final=(l == depth - 1))
    return x
```

```python
import functools
import math

import jax
import jax.numpy as jnp
import numpy as np
from jax import lax
from jax.experimental import pallas as pl
from jax.experimental.pallas import tpu as pltpu

F32 = jnp.float32
BF16 = jnp.bfloat16

D_MODEL = 1024
DN_HEADS = 8
DN_DK = 128
DN_DV = 128
DN_CONV = 4
DN_CHUNK = 64
DIL_GROUPS = ((128, 1), (512, 4), (2048, 16))
DIL_HEADS = 4
DIL_DH = 128
ATT_BLOCK = 128
NORM_EPS = 1e-6
N_DIL = len(DIL_GROUPS)
DN_W = DN_HEADS * DN_DK
DIL_W = DIL_HEADS * DIL_DH
GROUP_W = 3 * DIL_W
PROJ_SIZES = (DN_W, DN_W, DN_W, DN_W, DN_HEADS, DN_HEADS,
              N_DIL * DIL_W, N_DIL * DIL_W, N_DIL * DIL_W, DIL_W, D_MODEL, D_MODEL)

LANES = 128
VMEM_LIMIT = 56 * 1024 * 1024

NORM_TM = 512
PROJ_TM = 2048
PROJ_TN = 1024
PROJ_SUB = 512
CONV_SUB = 256
DN_TB = 512
DN_UNIT = 2
OUT_TM = 1024


def _sigmoid(x):
    return 1.0 / (1.0 + jnp.exp(-x))


def _silu(x):
    return x * _sigmoid(x)


def _softplus(x):
    return jnp.maximum(x, 0.0) + jnp.log1p(jnp.exp(-jnp.abs(x)))


def _mm(a, b):
    return lax.dot_general(a.astype(BF16), b.astype(BF16), (((1,), (0,)), ((), ())),
                           preferred_element_type=F32)


def _mm_nt(a, b):
    return lax.dot_general(a.astype(BF16), b.astype(BF16), (((1,), (1,)), ((), ())),
                           preferred_element_type=F32)


def _mm_tn(a, b):
    return lax.dot_general(a.astype(BF16), b.astype(BF16), (((0,), (0,)), ((), ())),
                           preferred_element_type=F32)


def _norm_kernel(x_ref, w_ref, ws_ref, h_ref, h4_ref, h16_ref, sm_ref, slab_ref):
    x = x_ref[...]
    y = x * lax.rsqrt(jnp.mean(x * x, axis=-1, keepdims=True) + NORM_EPS) * w_ref[...]
    hb = y.astype(BF16)
    h_ref[...] = hb
    sm_ref[...] = jnp.dot(hb, ws_ref[...], preferred_element_type=F32)
    n_slabs = D_MODEL // LANES
    for c in range(n_slabs):
        slab_ref[c] = y[:, c * LANES:(c + 1) * LANES]
    tm = x.shape[0]
    for d, out in ((4, h4_ref), (16, h16_ref)):
        n = tm // d
        for r in range(d):
            for c in range(n_slabs):
                out[r, :, c * LANES:(c + 1) * LANES] = slab_ref[c, pl.ds(r, n, stride=d), :].astype(BF16)


def _norm_call(x, w, w_small):
    B, S, D = x.shape
    tm = NORM_TM
    nt = S // tm
    return pl.pallas_call(
        _norm_kernel,
        grid=(B, nt),
        in_specs=[pl.BlockSpec((None, tm, D), lambda b, i: (b, i, 0)),
                  pl.BlockSpec((1, D), lambda b, i: (0, 0)),
                  pl.BlockSpec((D, LANES), lambda b, i: (0, 0))],
        out_specs=[pl.BlockSpec((None, tm, D), lambda b, i: (b, i, 0)),
                   pl.BlockSpec((None, 4, tm // 4, D), lambda b, i: (b, 0, i, 0)),
                   pl.BlockSpec((None, 16, tm // 16, D), lambda b, i: (b, 0, i, 0)),
                   pl.BlockSpec((None, tm, LANES), lambda b, i: (b, i, 0))],
        out_shape=[jax.ShapeDtypeStruct((B, S, D), BF16),
                   jax.ShapeDtypeStruct((B, 4, S // 4, D), BF16),
                   jax.ShapeDtypeStruct((B, 16, S // 16, D), BF16),
                   jax.ShapeDtypeStruct((B, S, LANES), F32)],
        scratch_shapes=[pltpu.VMEM((D // LANES, tm, LANES), F32)],
        compiler_params=pltpu.CompilerParams(dimension_semantics=("arbitrary", "arbitrary"),
                                             vmem_limit_bytes=VMEM_LIMIT),
        name="norm_deinterleave",
    )(x, w.reshape(1, D), w_small)


_ACTS = {"none": lambda v: v, "silu": _silu, "sigmoid": _sigmoid}


def _proj_kernel(h_ref, w_ref, o_ref, *, tiles):
    tm = h_ref.shape[0]
    sub = PROJ_SUB

    def run(parts):
        for m in range(tm // sub):
            rows = slice(m * sub, (m + 1) * sub)
            acc = jnp.dot(h_ref[rows, :], w_ref[...], preferred_element_type=F32)
            for lo, hi, act in parts:
                o_ref[rows, lo:hi] = _ACTS[act](acc[:, lo:hi]).astype(o_ref.dtype)

    kinds = sorted(set(tiles), key=tiles.index)
    if len(kinds) == 1:
        run(kinds[0])
        return
    j = pl.program_id(1)
    for parts in kinds:
        cond = functools.reduce(lambda a, b: a | b, [j == t for t, p in enumerate(tiles) if p == parts])
        pl.when(cond)(functools.partial(run, parts))


def _proj_call(h2d, w, tn, tiles, out_dtype, name):
    T, D = h2d.shape
    N = w.shape[1]
    tm = PROJ_TM
    assert len(tiles) * tn == N
    return pl.pallas_call(
        functools.partial(_proj_kernel, tiles=tiles),
        grid=(T // tm, N // tn),
        in_specs=[pl.BlockSpec((tm, D), lambda i, j: (i, 0)),
                  pl.BlockSpec((D, tn), lambda i, j: (0, j))],
        out_specs=pl.BlockSpec((tm, tn), lambda i, j: (i, j)),
        out_shape=jax.ShapeDtypeStruct((T, N), out_dtype),
        compiler_params=pltpu.CompilerParams(dimension_semantics=("arbitrary", "arbitrary"),
                                             vmem_limit_bytes=VMEM_LIMIT),
        name=name,
    )(h2d, w)


def _proj_conv_kernel(h_ref, w_ref, cw_ref, o_ref, tail_ref, *, tiles_per_seq):
    i = pl.program_id(1)
    tm = h_ref.shape[0]
    sub = CONV_SUB

    @pl.when(i % tiles_per_seq == 0)
    def _():
        tail_ref[...] = jnp.zeros_like(tail_ref)

    prev = tail_ref[...]
    taps = [cw_ref[k:k + 1, :] for k in range(DN_CONV)]
    for m in range(tm // sub):
        acc = jnp.dot(h_ref[m * sub:(m + 1) * sub, :], w_ref[...], preferred_element_type=F32)
        ext = jnp.concatenate([prev, acc], axis=0)
        y = taps[DN_CONV - 1] * acc
        for k in range(DN_CONV - 1):
            y = y + taps[k] * pltpu.roll(ext, DN_CONV - 1 - k, axis=0)[8:, :]
        o_ref[m * sub:(m + 1) * sub, :] = _silu(y).astype(o_ref.dtype)
        prev = acc[sub - 8:sub, :]
    tail_ref[...] = prev


def _proj_conv_call(h2d, w, conv_w, S, name):
    T, D = h2d.shape
    N = w.shape[1]
    tm, tn = PROJ_TM, PROJ_TN
    assert N % tn == 0 and T % tm == 0 and S % tm == 0
    cw = jnp.zeros((8, N), F32).at[0:DN_CONV].set(conv_w.astype(F32))
    return pl.pallas_call(
        functools.partial(_proj_conv_kernel, tiles_per_seq=S // tm),
        grid=(N // tn, T // tm),
        in_specs=[pl.BlockSpec((tm, D), lambda j, i: (i, 0)),
                  pl.BlockSpec((D, tn), lambda j, i: (0, j)),
                  pl.BlockSpec((8, tn), lambda j, i: (0, j))],
        out_specs=pl.BlockSpec((tm, tn), lambda j, i: (i, j)),
        out_shape=jax.ShapeDtypeStruct((T, N), BF16),
        scratch_shapes=[pltpu.VMEM((8, tn), F32)],
        compiler_params=pltpu.CompilerParams(dimension_semantics=("arbitrary", "arbitrary"),
                                             vmem_limit_bytes=VMEM_LIMIT),
        name=name,
    )(h2d, w, cw)


def _deltanet_kernel(qkv_ref, sm_ref, z_ref, hp_ref, nw_ref, o_ref, st_ref):
    C = DN_CHUNK
    tb = o_ref.shape[0]

    @pl.when(pl.program_id(1) == 0)
    def _():
        st_ref[...] = jnp.zeros_like(st_ref)

    row = lax.broadcasted_iota(jnp.int32, (C, C), 0)
    col = lax.broadcasted_iota(jnp.int32, (C, C), 1)
    causal = row >= col
    strict = row > col
    eye =(row == col).astype(F32)
    hp = hp_ref[...]
    neg_a = -jnp.exp(hp[0:1, :])
    dt_b = hp[1:2, :]
    nw = nw_ref[...]

    heads = range(DN_HEADS)

    def bcast(a, lane):
        return jnp.broadcast_to(a[:, lane:lane + 1], (C, LANES))

    def load(base, c0):
        return qkv_ref[base:base + C, c0:c0 + LANES].astype(F32)

    row_l = lax.broadcasted_iota(jnp.int32, (C, LANES), 0)

    def cumsum_rows(v):
        shift = 1
        while shift < C:
            v = v + jnp.where(row_l >= shift, pltpu.roll(v, shift, axis=0), 0.0)
            shift *= 2
        return v

    def prologue(unit, out):
        dec = {}
        for c in unit:
            sm = sm_ref[c * C:(c + 1) * C, :]
            g_all = neg_a * _softplus(sm + dt_b)
            gc_all = cumsum_rows(g_all)
            gc_t = jnp.concatenate([gc_all, jnp.zeros_like(gc_all)], axis=0).T
            glast_all = gc_all[C - 1:C, :]
            dec[c] = dict(beta=_sigmoid(sm), gc=gc_all, gc_t=gc_t, eg=jnp.exp(gc_all),
                          ek=jnp.exp(glast_all - gc_all), dl=jnp.exp(glast_all))
            yield
        q, k, kb, eg, gamma, rhs = [], [], [], [], [], []
        for c in unit:
            for h in heads:
                lane = DN_HEADS + h
                base = c * C
                qh = load(base, h * DN_DK)
                kh = load(base, DN_W + h * DN_DK)
                vh = load(base, 2 * DN_W + h * DN_DV)
                qh = qh * (lax.rsqrt(jnp.sum(qh * qh, axis=-1, keepdims=True) + NORM_EPS) * (DN_DK ** -0.5))
                kh = kh * lax.rsqrt(jnp.sum(kh * kh, axis=-1, keepdims=True) + NORM_EPS)
                beta = bcast(dec[c]["beta"], h)
                egh = bcast(dec[c]["eg"], lane)
                kbh = kh * beta
                q.append(qh)
                k.append(kh)
                kb.append(kbh)
                eg.append(egh)
                rhs.append(jnp.concatenate([vh * beta, kbh * egh], axis=1))
                diff = bcast(dec[c]["gc"], lane)[:, 0:C] - dec[c]["gc_t"][lane:lane + 1, 0:C]
                gamma.append(jnp.exp(jnp.where(causal, diff, -jnp.inf)))
                if h % 2 == 1:
                    yield
        out.update(dec=dec, q=q, k=k, kb=kb, eg=eg, gamma=gamma, rhs=rhs)

    def solve(unit, pro, out):
        idx = range(len(unit) * DN_HEADS)
        q, k, kb, gamma, rhs = pro["q"], pro["k"], pro["kb"], pro["gamma"], pro["rhs"]
        a = [_mm_nt(jnp.concatenate([kb[i], q[i]], axis=0), k[i]) for i in idx]
        yield
        p = [jnp.where(strict, -(a[i][0:C] * gamma[i]), 0.0) for i in idx]
        aqk = [a[i][C:] * gamma[i] for i in idx]
        t = [eye + p[i] for i in idx]
        p = [_mm(p[i], p[i]) for i in idx]
        yield
        for _ in range(4):
            r = [_mm(jnp.concatenate([p[i], t[i]], axis=0), p[i]) for i in idx]
            t = [t[i] + r[i][C:] for i in idx]
            p = [r[i][0:C] for i in idx]
            yield
        t = [t[i] + _mm(t[i], p[i]) for i in idx]
        yield
        out.update(x=[_mm(t[i], rhs[i]) for i in idx], aqk=aqk)

    state = [st_ref[h] for h in heads]

    def recur(unit, pro, sol):
        for n, c in enumerate(unit):
            base = c * C
            at = lambda lst, h: lst[n * DN_HEADS + h]
            ws_qs = [_mm(jnp.concatenate([at(sol["x"], h)[:, DN_DV:], at(pro["q"], h) * at(pro["eg"], h)], axis=0),
                         state[h]) for h in heads]
            yield
            v_new = [at(sol["x"], h)[:, 0:DN_DV] - ws_qs[h][0:C] for h in heads]
            o = [ws_qs[h][C:] + _mm(at(sol["aqk"], h), v_new[h]) for h in heads]
            for h in heads:
                lane = DN_HEADS + h
                kd = at(pro["k"], h) * bcast(pro["dec"][c]["ek"], lane)
                dl = jnp.broadcast_to(pro["dec"][c]["dl"][:, lane:lane + 1], (DN_DK, DN_DV))
                state[h] = state[h] * dl + _mm_tn(kd, v_new[h])
            yield
            for h in heads:
                y = o[h] * lax.rsqrt(jnp.mean(o[h] * o[h], axis=-1, keepdims=True) + NORM_EPS) * nw
                gate = z_ref[base:base + C, h * DN_DV:(h + 1) * DN_DV].astype(F32)
                o_ref[base:base + C, h * DN_DV:(h + 1) * DN_DV] = (y * gate).astype(o_ref.dtype)
                if h % 4 == 3:
                    yield

    def interleave(gens):
        gens = list(gens)
        while gens:
            for g in list(gens):
                try:
                    next(g)
                except StopIteration:
                    gens.remove(g)

    n_chunks = tb // C
    units = [list(range(u, min(u + DN_UNIT, n_chunks))) for u in range(0, n_chunks, DN_UNIT)]
    pro = [dict() for _ in units]
    sol = [dict() for _ in units]
    for tick in range(len(units) + 2):
        active = []
        if 0 <= tick - 2 < len(units):
            active.append(recur(units[tick - 2], pro[tick - 2], sol[tick - 2]))
        if 0 <= tick - 1 < len(units):
            active.append(solve(units[tick - 1], pro[tick - 1], sol[tick - 1]))
        if tick < len(units):
            active.append(prologue(units[tick], pro[tick]))
        interleave(active)
    for h in heads:
        st_ref[h] = state[h]


def _deltanet_call(qkv, small, p2, a_log, dt_bias, dn_norm_w, B, S):
    tb = DN_TB
    nt = S // tb
    hp = jnp.zeros((8, LANES), F32)
    hp = hp.at[0, DN_HEADS:2 * DN_HEADS].set(a_log.astype(F32)).at[1, DN_HEADS:2 * DN_HEADS].set(dt_bias.astype(F32))
    nw = dn_norm_w.astype(F32).reshape(1, DN_DV)
    return pl.pallas_call(
        _deltanet_kernel,
        grid=(B, nt),
        in_specs=[pl.BlockSpec((tb, 3 * DN_W), lambda b, i: (b * nt + i, 0)),
                  pl.BlockSpec((tb, LANES), lambda b, i: (b * nt + i, 0)),
                  pl.BlockSpec((tb, DN_W), lambda b, i: (b * nt + i, 0)),
                  pl.BlockSpec((8, LANES), lambda b, i: (0, 0)),
                  pl.BlockSpec((1, DN_DV), lambda b, i: (0, 0))],
        out_specs=pl.BlockSpec((tb, DN_W), lambda b, i: (b * nt + i, 0)),
        out_shape=jax.ShapeDtypeStruct((B * S, DN_W), BF16),
        scratch_shapes=[pltpu.VMEM((DN_HEADS, DN_DK, DN_DV), F32)],
        compiler_params=pltpu.CompilerParams(dimension_semantics=("arbitrary", "arbitrary"),
                                             vmem_limit_bytes=VMEM_LIMIT),
        name="deltanet",
    )(qkv, small, p2, hp, nw)


def _alibi_slope(idx):
    n = N_DIL * DIL_HEADS
    return 2.0 ** (-8.0 * (idx + 1) / n)


def _attn_kernel(q0_ref, k0_ref, v0_ref, q1_ref, k1_ref, v1_ref, q2_ref, k2_ref, v2_ref, z_ref, o_ref,
                 num_ref, den_ref, mx_ref):
    Q = ATT_BLOCK
    hd = pl.program_id(1)
    span = DIL_GROUPS[0][0] // DIL_GROUPS[0][1]
    dist = (Q + lax.broadcasted_iota(jnp.int32, (Q, 2 * Q), 0) - lax.broadcasted_iota(jnp.int32, (Q, 2 * Q), 1))
    valid = (dist >= 0) & (dist <= span)
    distf = dist.astype(F32)
    scale = DIL_DH ** -0.5

    def slope_of(g):
        s = jnp.float32(_alibi_slope(g * DIL_HEADS))
        for h in range(1, DIL_HEADS):
            s = jnp.where(hd == h, jnp.float32(_alibi_slope(g * DIL_HEADS + h)), s)
        return s

    def blocks(items):
        s = [_mm_nt(q, k) * scale + bias for q, k, v, bias in items]
        mx = [jnp.max(si, axis=-1, keepdims=True) for si in s]
        p = [jnp.exp(si - mi) for si, mi in zip(s, mx)]
        den = [jnp.sum(pi, axis=-1, keepdims=True) for pi in p]
        num = [_mm(pi, it[2]) for pi, it in zip(p, items)]
        return [(n, jnp.broadcast_to(dn, (Q, LANES)), jnp.broadcast_to(m, (Q, LANES)))
                for n, dn, m in zip(num, den, mx)]

    def merge(a, b):
        (n1, d1, m1), (n2, d2, m2) = a, b
        m = jnp.maximum(m1, m2)
        a1 = jnp.exp(m1 - m)
        a2 = jnp.exp(m2 - m)
        return n1 * a1 + n2 * a2, d1 * a1 + d2 * a2, m

    def run_group(g, q_ref, k_ref, v_ref, emit, r_unroll, j_unroll):
        d = DIL_GROUPS[g][1]
        bias2 = jnp.where(valid, -(slope_of(g) * float(d)) * distf, -jnp.inf)
        bias1 = bias2[:, Q:]
        nb = q_ref.shape[-2] // Q
        j_unroll = min(j_unroll, nb)

        def refs(r):
            if d == 1:
                return q_ref, k_ref, v_ref
            return q_ref.at[r], k_ref.at[r], v_ref.at[r]

        def item(r, j):
            qr, kr, vr = refs(r)
            if isinstance(j, int) and j == 0:
                return qr[0:Q, :], kr[0:Q, :], vr[0:Q, :], bias1
            if isinstance(j, int):
                lo, cur = (j - 1) * Q, j * Q
            else:
                lo, cur = pl.multiple_of((j - 1) * Q, Q), pl.multiple_of(j * Q, Q)
            return qr[pl.ds(cur, Q), :], kr[pl.ds(lo, 2 * Q), :], vr[pl.ds(lo, 2 * Q), :], bias2

        def batch(rs, js):
            ids = [(r, j) for r in rs for j in js]
            for (r, j), res in zip(ids, blocks([item(r, j) for r, j in ids])):
                emit(r, j, res)

        def sub(rb, carry):
            rs = [rb * r_unroll + t for t in range(r_unroll)] if d > 1 else [0]
            batch(rs, list(range(j_unroll)))

            def jbatch(jb, c2):
                batch(rs, [jb * j_unroll + t for t in range(j_unroll)])
                return c2

            lax.fori_loop(1, nb // j_unroll, jbatch, 0)
            return carry

        if d == 1:
            sub(0, 0)
        else:
            lax.fori_loop(0, d // r_unroll, sub, 0)

    def rows(g, r, j):
        d = DIL_GROUPS[g][1]
        if d == 1:
            return pl.ds(j * Q, Q) if isinstance(j, int) else pl.ds(pl.multiple_of(j * Q, Q), Q)
        return pl.ds(j * (Q * d) + r, Q, stride=d)

    def emit_first(r, j, res):
        idx = rows(2, r, j)
        num_ref[idx, :] = res[0]
        den_ref[idx, :] = res[1]
        mx_ref[idx, :] = res[2]

    def emit_second(r, j, res):
        idx = rows(1, r, j)
        n, dn, m = merge((num_ref[idx, :], den_ref[idx, :], mx_ref[idx, :]), res)
        num_ref[idx, :] = n
        den_ref[idx, :] = dn
        mx_ref[idx, :] = m

    def emit_last(r, j, res):
        idx = rows(0, r, j)
        n, dn, _ = merge((num_ref[idx, :], den_ref[idx, :], mx_ref[idx, :]), res)
        o_ref[idx, :] = ((n / dn) * z_ref[idx, :].astype(F32)).astype(o_ref.dtype)

    run_group(2, q2_ref, k2_ref, v2_ref, emit_first, r_unroll=4, j_unroll=2)
    run_group(1, q1_ref, k1_ref, v1_ref, emit_second, r_unroll=1, j_unroll=8)
    run_group(0, q0_ref, k0_ref, v0_ref, emit_last, r_unroll=1, j_unroll=8)


def _attn_call(p2, off0, zb_off, g1, g2, B, S):
    H = DIL_HEADS
    p2v = p2.reshape(B, S, p2.shape[1])
    g1v = g1.reshape(B, 4, S // 4, GROUP_W)
    g2v = g2.reshape(B, 16, S // 16, GROUP_W)
    b0 = off0 // LANES
    zb0 = zb_off // LANES

    def nat(cb):
        return pl.BlockSpec((None, S, LANES), lambda b, h: (b, 0, cb + h))

    def perm(d, cb):
        return pl.BlockSpec((None, d, S // d, LANES), lambda b, h: (b, 0, 0, cb + h))

    return pl.pallas_call(
        _attn_kernel,
        grid=(B, H),
        in_specs=[nat(b0), nat(b0 + H), nat(b0 + 2 * H),
                  perm(4, 0), perm(4, H), perm(4, 2 * H),
                  perm(16, 0), perm(16, H), perm(16, 2 * H),
                  nat(zb0)],
        out_specs=pl.BlockSpec((None, S, LANES), lambda b, h: (b, 0, h)),
        out_shape=jax.ShapeDtypeStruct((B, S, DIL_W), BF16),
        scratch_shapes=[pltpu.VMEM((S, LANES), F32)] * 3,
        compiler_params=pltpu.CompilerParams(dimension_semantics=("arbitrary", "arbitrary"),
                                             vmem_limit_bytes=VMEM_LIMIT),
        name="dilated_attention",
    )(p2v, p2v, p2v, g1v, g1v, g1v, g2v, g2v, g2v, p2v)


def _out_kernel(oa_ref, ob_ref, ga_ref, gb_ref, x_ref, wa_ref, wb_ref, wo_ref, fw_ref, o_ref, *, final):
    ya = jnp.dot(oa_ref[...], wa_ref[...], preferred_element_type=F32)
    yb = jnp.dot(ob_ref[...], wb_ref[...], preferred_element_type=F32)
    merged = ga_ref[...].astype(F32) * ya + gb_ref[...].astype(F32) * yb
    xn = x_ref[...] + jnp.dot(merged.astype(BF16), wo_ref[...], preferred_element_type=F32)
    if final:
        xn = xn * lax.rsqrt(jnp.mean(xn * xn, axis=-1, keepdims=True) + NORM_EPS) * fw_ref[...]
    o_ref[...] = xn


def _out_call(oa, ob, p2, ga_off, gb_off, x2d, wa, wb, wo, fw, final):
    T, D = x2d.shape
    tm = OUT_TM
    ga_b = ga_off // D
    gb_b = gb_off // D
    const = lambda i: (0, 0)
    return pl.pallas_call(
        functools.partial(_out_kernel, final=final),
        grid=(T // tm,),
        in_specs=[pl.BlockSpec((tm, DN_W), lambda i: (i, 0)),
                  pl.BlockSpec((tm, DIL_W), lambda i: (i, 0)),
                  pl.BlockSpec((tm, D), lambda i: (i, ga_b)),
                  pl.BlockSpec((tm, D), lambda i: (i, gb_b)),
                  pl.BlockSpec((tm, D), lambda i: (i, 0)),
                  pl.BlockSpec((DN_W, D), const),
                  pl.BlockSpec((DIL_W, D), const),
                  pl.BlockSpec((D, D), const),
                  pl.BlockSpec((1, D), const)],
        out_specs=pl.BlockSpec((tm, D), lambda i: (i, 0)),
        out_shape=jax.ShapeDtypeStruct((T, D), F32),
        compiler_params=pltpu.CompilerParams(dimension_semantics=("arbitrary",),
                                             vmem_limit_bytes=VMEM_LIMIT),
        name="out_merge",
    )(oa, ob, p2, p2, x2d, wa, wb, wo, fw)


def _split_cols(w):
    out, start = [], 0
    for s in PROJ_SIZES:
        out.append(w[:, start:start + s])
        start += s
    return out


def _layer(x, norm_w, w_in, conv_w, a_log, dt_bias, dn_norm_w, w_o_dn, w_o_dil, w_out, final_w, final):
    B, S, D = x.shape
    T = B * S
    (wq_a, wk_a, wv_a, wz_a, wb_a, wa_a, wq_b, wk_b, wv_b, wz_b, wg_a, wg_b) = _split_cols(w_in)
    w_small = jnp.concatenate([wb_a, wa_a, jnp.zeros((D, LANES - 2 * DN_HEADS), w_in.dtype)], axis=1).astype(BF16)
    w1 = jnp.concatenate([wq_a, wk_a, wv_a], axis=1).astype(BF16)

    def group_w(g):
        sl = slice(g * DIL_W, (g + 1) * DIL_W)
        return jnp.concatenate([wq_b[:, sl], wk_b[:, sl], wv_b[:, sl]], axis=1)

    w2 = jnp.concatenate([wz_a, wg_a, wg_b, wz_b, group_w(0)], axis=1).astype(BF16)
    ga_off = DN_W
    gb_off = ga_off + D
    zb_off = gb_off + D
    off0 = zb_off + DIL_W
    tn = D
    tiles2 = (((0, tn, "silu"),), ((0, tn, "sigmoid"),), ((0, tn, "sigmoid"),),
              ((0, DIL_W, "silu"), (DIL_W, tn, "none")), ((0, tn, "none"),))
    plain = (((0, GROUP_W, "none"),),)

    h, h4, h16, small = _norm_call(x, norm_w.astype(F32), w_small)
    qkv = _proj_conv_call(h.reshape(T, D), w1, conv_w, S, "proj_conv_deltanet")
    p2 = _proj_call(h.reshape(T, D), w2, tn, tiles2, BF16, "proj_gates_group0")
    g1 = _proj_call(h4.reshape(T, D), group_w(1).astype(BF16), GROUP_W, plain, BF16, "proj_group1")
    g2 = _proj_call(h16.reshape(T, D), group_w(2).astype(BF16), GROUP_W, plain, BF16, "proj_group2")

    o_a = _deltanet_call(qkv, small.reshape(T, LANES), p2, a_log, dt_bias, dn_norm_w, B, S)
    o_b = _attn_call(p2, off0, zb_off, g1, g2, B, S)
    out = _out_call(o_a, o_b.reshape(T, DIL_W), p2, ga_off, gb_off, x.reshape(T, D),
                    w_o_dn.astype(BF16), w_o_dil.astype(BF16), w_out.astype(BF16),
                    final_w.astype(F32).reshape(1, D), final)
    return out.reshape(B, S, D)


def kernel(x, norm_w, w_in, conv_w, a_log, dt_bias, dn_norm_w, w_o_dn, w_o_dil, w_out, final_norm_w):
    depth = norm_w.shape[0]
    for l in range(depth):
        x = _layer(x, norm_w[l], w_in[l], conv_w[l], a_log[l], dt_bias[l], dn_norm_w[l],
                   w_o_dn[l], w_o_dil[l], w_out[l], final_norm_w, final=(l == depth - 1))
    return x
```

```python
import functools
import math

import jax
import jax.numpy as jnp
import numpy as np
from jax import lax
from jax.experimental import pallas as pl
from jax.experimental.pallas import tpu as pltpu

F32 = jnp.float32
BF16 = jnp.bfloat16

D_MODEL = 1024
DN_HEADS = 8
DN_DK = 128
DN_DV = 128
DN_CONV = 4
DN_CHUNK = 64
DIL_GROUPS = ((128, 1), (512, 4), (2048, 16))
DIL_HEADS = 4
DIL_DH = 128
ATT_BLOCK = 128
NORM_EPS = 1e-6
N_DIL = len(DIL_GROUPS)
DN_W = DN_HEADS * DN_DK
DIL_W = DIL_HEADS * DIL_DH
GROUP_W = 3 * DIL_W
PROJ_SIZES = (DN_W, DN_W, DN_W, DN_W, DN_HEADS, DN_HEADS,
              N_DIL * DIL_W, N_DIL * DIL_W, N_DIL * DIL_W, DIL_W, D_MODEL, D_MODEL)

LANES = 128
VMEM_LIMIT = 56 * 1024 * 1024

NORM_TM = 512
PROJ_TM = 2048
PROJ_TN = 1024
PROJ_SUB = 512
CONV_SUB = 256
DN_TB = 512
DN_UNIT = 2
OUT_TM = 1024


def _sigmoid(x):
    return 1.0 / (1.0 + jnp.exp(-x))


def _silu(x):
    return x * _sigmoid(x)


def _softplus(x):
    return jnp.maximum(x, 0.0) + jnp.log1p(jnp.exp(-jnp.abs(x)))


def _mm(a, b):
    return lax.dot_general(a.astype(BF16), b.astype(BF16), (((1,), (0,)), ((), ())),
                           preferred_element_type=F32)


def _mm_nt(a, b):
    return lax.dot_general(a.astype(BF16), b.astype(BF16), (((1,), (1,)), ((), ())),
                           preferred_element_type=F32)


def _mm_tn(a, b):
    return lax.dot_general(a.astype(BF16), b.astype(BF16), (((0,), (0,)), ((), ())),
                           preferred_element_type=F32)


def _norm_kernel(x_ref, w_ref, ws_ref, h_ref, h4_ref, h16_ref, sm_ref, slab_ref):
    x = x_ref[...]
    y = x * lax.rsqrt(jnp.mean(x * x, axis=-1, keepdims=True) + NORM_EPS) * w_ref[...]
    hb = y.astype(BF16)
    h_ref[...] = hb
    sm_ref[...] = jnp.dot(hb, ws_ref[...], preferred_element_type=F32)
    n_slabs = D_MODEL // LANES
    for c in range(n_slabs):
        slab_ref[c] = y[:, c * LANES:(c + 1) * LANES]
    tm = x.shape[0]
    for d, out in ((4, h4_ref), (16, h16_ref)):
        n = tm // d
        for r in range(d):
            for c in range(n_slabs):
                out[r, :, c * LANES:(c + 1) * LANES] = slab_ref[c, pl.ds(r, n, stride=d), :].astype(BF16)


def _norm_call(x, w, w_small):
    B, S, D = x.shape
    tm = NORM_TM
    nt = S // tm
    return pl.pallas_call(
        _norm_kernel,
        grid=(B, nt),
        in_specs=[pl.BlockSpec((None, tm, D), lambda b, i: (b, i, 0)),
                  pl.BlockSpec((1, D), lambda b, i: (0, 0)),
                  pl.BlockSpec((D, LANES), lambda b, i: (0, 0))],
        out_specs=[pl.BlockSpec((None, tm, D), lambda b, i: (b, i, 0)),
                   pl.BlockSpec((None, 4, tm // 4, D), lambda b, i: (b, 0, i, 0)),
                   pl.BlockSpec((None, 16, tm // 16, D), lambda b, i: (b, 0, i, 0)),
                   pl.BlockSpec((None, tm, LANES), lambda b, i: (b, i, 0))],
        out_shape=[jax.ShapeDtypeStruct((B, S, D), BF16),
                   jax.ShapeDtypeStruct((B, 4, S // 4, D), BF16),
                   jax.ShapeDtypeStruct((B, 16, S // 16, D), BF16),
                   jax.ShapeDtypeStruct((B, S, LANES), F32)],
        scratch_shapes=[pltpu.VMEM((D // LANES, tm, LANES), F32)],
        compiler_params=pltpu.CompilerParams(dimension_semantics=("arbitrary", "arbitrary"),
                                             vmem_limit_bytes=VMEM_LIMIT),
        name="norm_deinterleave",
    )(x, w.reshape(1, D), w_small)


_ACTS = {"none": lambda v: v, "silu": _silu, "sigmoid": _sigmoid}


def _proj_kernel(h_ref, w_ref, o_ref, *, tiles):
    tm = h_ref.shape[0]
    sub = PROJ_SUB

    def run(parts):
        for m in range(tm // sub):
            rows = slice(m * sub, (m + 1) * sub)
            acc = jnp.dot(h_ref[rows, :], w_ref[...], preferred_element_type=F32)
            for lo, hi, act in parts:
                o_ref[rows, lo:hi] = _ACTS[act](acc[:, lo:hi]).astype(o_ref.dtype)

    kinds = sorted(set(tiles), key=tiles.index)
    if len(kinds) == 1:
        run(kinds[0])
        return
    j = pl.program_id(1)
    for parts in kinds:
        cond = functools.reduce(lambda a, b: a | b, [j == t for t, p in enumerate(tiles) if p == parts])
        pl.when(cond)(functools.partial(run, parts))


def _proj_call(h2d, w, tn, tiles, out_dtype, name):
    T, D = h2d.shape
    N = w.shape[1]
    tm = PROJ_TM
    assert len(tiles) * tn == N
    return pl.pallas_call(
        functools.partial(_proj_kernel, tiles=tiles),
        grid=(T // tm, N // tn),
        in_specs=[pl.BlockSpec((tm, D), lambda i, j: (i, 0)),
                  pl.BlockSpec((D, tn), lambda i, j: (0, j))],
        out_specs=pl.BlockSpec((tm, tn), lambda i, j: (i, j)),
        out_shape=jax.ShapeDtypeStruct((T, N), out_dtype),
        compiler_params=pltpu.CompilerParams(dimension_semantics=("arbitrary", "arbitrary"),
                                             vmem_limit_bytes=VMEM_LIMIT),
        name=name,
    )(h2d, w)


def _proj_conv_kernel(h_ref, w_ref, cw_ref, o_ref, tail_ref, *, tiles_per_seq):
    i = pl.program_id(1)
    tm = h_ref.shape[0]
    sub = CONV_SUB

    @pl.when(i % tiles_per_seq == 0)
    def _():
        tail_ref[...] = jnp.zeros_like(tail_ref)

    assert DN_CONV == 4
    prev = tail_ref[...]
    taps = [cw_ref[k:k + 1, :] for k in range(DN_CONV)]
    for m in range(tm // sub):
        acc = jnp.dot(h_ref[m * sub:(m + 1) * sub, :], w_ref[...], preferred_element_type=F32)
        ext = jnp.concatenate([prev, acc], axis=0)
        ext1 = pltpu.roll(ext, 1, axis=0)
        b = taps[1] * ext + taps[0] * ext1
        y = taps[3] * acc + taps[2] * ext1[8:, :] + pltpu.roll(b, 2, axis=0)[8:, :]
        o_ref[m * sub:(m + 1) * sub, :] = _silu(y).astype(o_ref.dtype)
        prev = acc[sub - 8:sub, :]
    tail_ref[...] = prev


def _proj_conv_call(h2d, w, conv_w, S, name):
    T, D = h2d.shape
    N = w.shape[1]
    tm, tn = PROJ_TM, PROJ_TN
    assert N % tn == 0 and T % tm == 0 and S % tm == 0
    cw = jnp.zeros((8, N), F32).at[0:DN_CONV].set(conv_w.astype(F32))
    return pl.pallas_call(
        functools.partial(_proj_conv_kernel, tiles_per_seq=S // tm),
        grid=(N // tn, T // tm),
        in_specs=[pl.BlockSpec((tm, D), lambda j, i: (i, 0)),
                  pl.BlockSpec((D, tn), lambda j, i: (0, j)),
                  pl.BlockSpec((8, tn), lambda j, i: (0, j))],
        out_specs=pl.BlockSpec((tm, tn), lambda j, i: (i, j)),
        out_shape=jax.ShapeDtypeStruct((T, N), BF16),
        scratch_shapes=[pltpu.VMEM((8, tn), F32)],
        compiler_params=pltpu.CompilerParams(dimension_semantics=("arbitrary", "arbitrary"),
                                             vmem_limit_bytes=VMEM_LIMIT),
        name=name,
    )(h2d, w, cw)


def _deltanet_kernel(qkv_ref, sm_ref, z_ref, hp_ref, nw_ref, o_ref, st_ref):
    C = DN_CHUNK
    tb = o_ref.shape[0]

    @pl.when(pl.program_id(1) == 0)
    def _():
        st_ref[...] = jnp.zeros_like(st_ref)

    row = lax.broadcasted_iota(jnp.int32, (C, C), 0)
    col = lax.broadcasted_iota(jnp.int32, (C, C), 1)
    causal = row >= col
    strict = row > col
    eye =(row == col).astype(F32)
    hp = hp_ref[...]
    neg_a = -jnp.exp(hp[0:1, :])
    dt_b = hp[1:2, :]
    nw = nw_ref[...]

    heads = range(DN_HEADS)

    def bcast(a, lane):
        return jnp.broadcast_to(a[:, lane:lane + 1], (C, LANES))

    def load(base, c0):
        return qkv_ref[base:base + C, c0:c0 + LANES].astype(F32)

    row_l = lax.broadcasted_iota(jnp.int32, (C, LANES), 0)

    def cumsum_rows(v):
        shift = 1
        while shift < C:
            v = v + jnp.where(row_l >= shift, pltpu.roll(v, shift, axis=0), 0.0)
            shift *= 2
        return v

    def prologue(unit, out):
        dec = {}
        for c in unit:
            sm = sm_ref[c * C:(c + 1) * C, :]
            g_all = neg_a * _softplus(sm + dt_b)
            gc_all = cumsum_rows(g_all)
            gc_t = jnp.concatenate([gc_all, jnp.zeros_like(gc_all)], axis=0).T
            glast_all = gc_all[C - 1:C, :]
            dec[c] = dict(beta=_sigmoid(sm), gc=gc_all, gc_t=gc_t, eg=jnp.exp(gc_all),
                          ek=jnp.exp(glast_all - gc_all), dl=jnp.exp(glast_all))
            yield
        q, k, kb, eg, gamma, rhs = [], [], [], [], [], []
        for c in unit:
            for h in heads:
                lane = DN_HEADS + h
                base = c * C
                qh = load(base, h * DN_DK)
                kh = load(base, DN_W + h * DN_DK)
                vh = load(base, 2 * DN_W + h * DN_DV)
                qh = qh * (lax.rsqrt(jnp.sum(qh * qh, axis=-1, keepdims=True) + NORM_EPS) * (DN_DK ** -0.5))
                kh = kh * lax.rsqrt(jnp.sum(kh * kh, axis=-1, keepdims=True) + NORM_EPS)
                beta = bcast(dec[c]["beta"], h)
                egh = bcast(dec[c]["eg"], lane)
                kbh = kh * beta
                q.append(qh)
                k.append(kh)
                kb.append(kbh)
                eg.append(egh)
                rhs.append(jnp.concatenate([vh * beta, kbh * egh], axis=1))
                diff = bcast(dec[c]["gc"], lane)[:, 0:C] - dec[c]["gc_t"][lane:lane + 1, 0:C]
                gamma.append(jnp.exp(jnp.where(causal, diff, -jnp.inf)))
                if h % 2 == 1:
                    yield
        out.update(dec=dec, q=q, k=k, kb=kb, eg=eg, gamma=gamma, rhs=rhs)

    def solve(unit, pro, out):
        idx = range(len(unit) * DN_HEADS)
        q, k, kb, gamma, rhs = pro["q"], pro["k"], pro["kb"], pro["gamma"], pro["rhs"]
        a = [_mm_nt(jnp.concatenate([kb[i], q[i]], axis=0), k[i]) for i in idx]
        yield
        p = [jnp.where(strict, -(a[i][0:C] * gamma[i]), 0.0) for i in idx]
        aqk = [a[i][C:] * gamma[i] for i in idx]
        t = [eye + p[i] for i in idx]
        p = [_mm(p[i], p[i]) for i in idx]
        yield
        for _ in range(4):
            r = [_mm(jnp.concatenate([p[i], t[i]], axis=0), p[i]) for i in idx]
            t = [t[i] + r[i][C:] for i in idx]
            p = [r[i][0:C] for i in idx]
            yield
        t = [t[i] + _mm(t[i], p[i]) for i in idx]
        yield
        out.update(x=[_mm(t[i], rhs[i]) for i in idx], aqk=aqk)

    state = [st_ref[h] for h in heads]

    def recur(unit, pro, sol):
        for n, c in enumerate(unit):
            base = c * C
            at = lambda lst, h: lst[n * DN_HEADS + h]
            ws_qs = [_mm(jnp.concatenate([at(sol["x"], h)[:, DN_DV:], at(pro["q"], h) * at(pro["eg"], h)], axis=0),
                         state[h]) for h in heads]
            yield
            v_new = [at(sol["x"], h)[:, 0:DN_DV] - ws_qs[h][0:C] for h in heads]
            o = [ws_qs[h][C:] + _mm(at(sol["aqk"], h), v_new[h]) for h in heads]
            for h in heads:
                lane = DN_HEADS + h
                kd = at(pro["k"], h) * bcast(pro["dec"][c]["ek"], lane)
                dl = jnp.broadcast_to(pro["dec"][c]["dl"][:, lane:lane + 1], (DN_DK, DN_DV))
                state[h] = state[h] * dl + _mm_tn(kd, v_new[h])
            yield
            for h in heads:
                y = o[h] * lax.rsqrt(jnp.mean(o[h] * o[h], axis=-1, keepdims=True) + NORM_EPS) * nw
                gate = z_ref[base:base + C, h * DN_DV:(h + 1) * DN_DV].astype(F32)
                o_ref[base:base + C, h * DN_DV:(h + 1) * DN_DV] = (y * gate).astype(o_ref.dtype)
                if h % 4 == 3:
                    yield

    def interleave(gens):
        gens = list(gens)
        while gens:
            for g in list(gens):
                try:
                    next(g)
                except StopIteration:
                    gens.remove(g)

    n_chunks = tb // C
    units = [list(range(u, min(u + DN_UNIT, n_chunks))) for u in range(0, n_chunks, DN_UNIT)]
    pro = [dict() for _ in units]
    sol = [dict() for _ in units]
    for tick in range(len(units) + 2):
        active = []
        if 0 <= tick - 2 < len(units):
            active.append(recur(units[tick - 2], pro[tick - 2], sol[tick - 2]))
        if 0 <= tick - 1 < len(units):
            active.append(solve(units[tick - 1], pro[tick - 1], sol[tick - 1]))
        if tick < len(units):
            active.append(prologue(units[tick], pro[tick]))
        interleave(active)
    for h in heads:
        st_ref[h] = state[h]


def _deltanet_call(qkv, small, p2, a_log, dt_bias, dn_norm_w, B, S):
    tb = DN_TB
    nt = S // tb
    hp = jnp.zeros((8, LANES), F32)
    hp = hp.at[0, DN_HEADS:2 * DN_HEADS].set(a_log.astype(F32)).at[1, DN_HEADS:2 * DN_HEADS].set(dt_bias.astype(F32))
    nw = dn_norm_w.astype(F32).reshape(1, DN_DV)
    return pl.pallas_call(
        _deltanet_kernel,
        grid=(B, nt),
        in_specs=[pl.BlockSpec((tb, 3 * DN_W), lambda b, i: (b * nt + i, 0)),
                  pl.BlockSpec((tb, LANES), lambda b, i: (b * nt + i, 0)),
                  pl.BlockSpec((tb, DN_W), lambda b, i: (b * nt + i, 0)),
                  pl.BlockSpec((8, LANES), lambda b, i: (0, 0)),
                  pl.BlockSpec((1, DN_DV), lambda b, i: (0, 0))],
        out_specs=pl.BlockSpec((tb, DN_W), lambda b, i: (b * nt + i, 0)),
        out_shape=jax.ShapeDtypeStruct((B * S, DN_W), BF16),
        scratch_shapes=[pltpu.VMEM((DN_HEADS, DN_DK, DN_DV), F32)],
        compiler_params=pltpu.CompilerParams(dimension_semantics=("arbitrary", "arbitrary"),
                                             vmem_limit_bytes=VMEM_LIMIT),
        name="deltanet",
    )(qkv, small, p2, hp, nw)


def _alibi_slope(idx):
    n = N_DIL * DIL_HEADS
    return 2.0 ** (-8.0 * (idx + 1) / n)


def _attn_kernel(q0_ref, k0_ref, v0_ref, q1_ref, k1_ref, v1_ref, q2_ref, k2_ref, v2_ref, z_ref, o_ref,
                 num_ref, den_ref, mx_ref):
    Q = ATT_BLOCK
    hd = pl.program_id(1)
    span = DIL_GROUPS[0][0] // DIL_GROUPS[0][1]
    dist = (Q + lax.broadcasted_iota(jnp.int32, (Q, 2 * Q), 0) - lax.broadcasted_iota(jnp.int32, (Q, 2 * Q), 1))
    valid = (dist >= 0) & (dist <= span)
    distf = dist.astype(F32)
    scale = DIL_DH ** -0.5

    def slope_of(g):
        s = jnp.float32(_alibi_slope(g * DIL_HEADS))
        for h in range(1, DIL_HEADS):
            s = jnp.where(hd == h, jnp.float32(_alibi_slope(g * DIL_HEADS + h)), s)
        return s

    def blocks(items):
        s = [_mm_nt(q, k) * scale + bias for q, k, v, bias in items]
        mx = [jnp.max(si, axis=-1, keepdims=True) for si in s]
        p = [jnp.exp(si - mi) for si, mi in zip(s, mx)]
        den = [jnp.sum(pi, axis=-1, keepdims=True) for pi in p]
        num = [_mm(pi, it[2]) for pi, it in zip(p, items)]
        return [(n, jnp.broadcast_to(dn, (Q, LANES)), jnp.broadcast_to(m, (Q, LANES)))
                for n, dn, m in zip(num, den, mx)]

    def run_group(g, q_ref, k_ref, v_ref, emit, r_unroll, j_unroll):
        d = DIL_GROUPS[g][1]
        bias2 = jnp.where(valid, -(slope_of(g) * float(d)) * distf, -jnp.inf)
        bias1 = bias2[:, Q:]
        nb = q_ref.shape[-2] // Q
        j_unroll = min(j_unroll, nb)

        def refs(r):
            if d == 1:
                return q_ref, k_ref, v_ref
            return q_ref.at[r], k_ref.at[r], v_ref.at[r]

        def item(r, j):
            qr, kr, vr = refs(r)
            if isinstance(j, int) and j == 0:
                return qr[0:Q, :], kr[0:Q, :], vr[0:Q, :], bias1
            if isinstance(j, int):
                lo, cur = (j - 1) * Q, j * Q
            else:
                lo, cur = pl.multiple_of((j - 1) * Q, Q), pl.multiple_of(j * Q, Q)
            return qr[pl.ds(cur, Q), :], kr[pl.ds(lo, 2 * Q), :], vr[pl.ds(lo, 2 * Q), :], bias2

        def batch(rs, js):
            ids = [(r, j) for r in rs for j in js]
            for (r, j), res in zip(ids, blocks([item(r, j) for r, j in ids])):
                emit(r, j, res)

        def sub(rb, carry):
            rs = [rb * r_unroll + t for t in range(r_unroll)] if d > 1 else [0]
            batch(rs, list(range(j_unroll)))

            def jbatch(jb, c2):
                batch(rs, [jb * j_unroll + t for t in range(j_unroll)])
                return c2

            lax.fori_loop(1, nb // j_unroll, jbatch, 0)
            return carry

        if d == 1:
            sub(0, 0)
        else:
            lax.fori_loop(0, d // r_unroll, sub, 0)

    def rows(g, r, j):
        d = DIL_GROUPS[g][1]
        if d == 1:
            return pl.ds(j * Q, Q) if isinstance(j, int) else pl.ds(pl.multiple_of(j * Q, Q), Q)
        return pl.ds(j * (Q * d) + r, Q, stride=d)

    def store_to(g, slot):
        def emit(r, j, res):
            idx = rows(g, r, j)
            num_ref[slot, idx, :] = res[0]
            den_ref[slot, idx, :] = res[1]
            mx_ref[slot, idx, :] = res[2]
        return emit

    def emit_last(r, j, res):
        idx = rows(0, r, j)
        parts = [(num_ref[s, idx, :], den_ref[s, idx, :], mx_ref[s, idx, :]) for s in range(2)] + [res]
        m = jnp.maximum(jnp.maximum(parts[0][2], parts[1][2]), parts[2][2])
        w = [jnp.exp(p[2] - m) for p in parts]
        n = parts[0][0] * w[0] + parts[1][0] * w[1] + parts[2][0] * w[2]
        dn = parts[0][1] * w[0] + parts[1][1] * w[1] + parts[2][1] * w[2]
        o_ref[idx, :] = ((n / dn) * z_ref[idx, :].astype(F32)).astype(o_ref.dtype)

    run_group(2, q2_ref, k2_ref, v2_ref, store_to(2, 0), r_unroll=4, j_unroll=2)
    run_group(1, q1_ref, k1_ref, v1_ref, store_to(1, 1), r_unroll=1, j_unroll=8)
    run_group(0, q0_ref, k0_ref, v0_ref, emit_last, r_unroll=1, j_unroll=8)


def _attn_call(p2, off0, zb_off, g1, g2, B, S):
    H = DIL_HEADS
    p2v = p2.reshape(B, S, p2.shape[1])
    g1v = g1.reshape(B, 4, S // 4, GROUP_W)
    g2v = g2.reshape(B, 16, S // 16, GROUP_W)
    b0 = off0 // LANES
    zb0 = zb_off // LANES

    def nat(cb):
        return pl.BlockSpec((None, S, LANES), lambda b, h: (b, 0, cb + h))

    def perm(d, cb):
        return pl.BlockSpec((None, d, S // d, LANES), lambda b, h: (b, 0, 0, cb + h))

    return pl.pallas_call(
        _attn_kernel,
        grid=(B, H),
        in_specs=[nat(b0), nat(b0 + H), nat(b0 + 2 * H),
                  perm(4, 0), perm(4, H), perm(4, 2 * H),
                  perm(16, 0), perm(16, H), perm(16, 2 * H),
                  nat(zb0)],
        out_specs=pl.BlockSpec((None, S, LANES), lambda b, h: (b, 0, h)),
        out_shape=jax.ShapeDtypeStruct((B, S, DIL_W), BF16),
        scratch_shapes=[pltpu.VMEM((2, S, LANES), F32)] * 3,
        compiler_params=pltpu.CompilerParams(dimension_semantics=("arbitrary", "arbitrary"),
                                             vmem_limit_bytes=VMEM_LIMIT),
        name="dilated_attention",
    )(p2v, p2v, p2v, g1v, g1v, g1v, g2v, g2v, g2v, p2v)


def _out_kernel(oa_ref, ob_ref, ga_ref, gb_ref, x_ref, wa_ref, wb_ref, wo_ref, fw_ref, o_ref, *, final):
    ya = jnp.dot(oa_ref[...], wa_ref[...], preferred_element_type=F32)
    yb = jnp.dot(ob_ref[...], wb_ref[...], preferred_element_type=F32)
    merged = ga_ref[...].astype(F32) * ya + gb_ref[...].astype(F32) * yb
    xn = x_ref[...] + jnp.dot(merged.astype(BF16), wo_ref[...], preferred_element_type=F32)
    if final:
        xn = xn * lax.rsqrt(jnp.mean(xn * xn, axis=-1, keepdims=True) + NORM_EPS) * fw_ref[...]
    o_ref[...] = xn


def _out_call(oa, ob, p2, ga_off, gb_off, x2d, wa, wb, wo, fw, final):
    T, D = x2d.shape
    tm = OUT_TM
    ga_b = ga_off // D
    gb_b = gb_off // D
    const = lambda i: (0, 0)
    return pl.pallas_call(
        functools.partial(_out_kernel, final=final),
        grid=(T // tm,),
        in_specs=[pl.BlockSpec((tm, DN_W), lambda i: (i, 0)),
                  pl.BlockSpec((tm, DIL_W), lambda i: (i, 0)),
                  pl.BlockSpec((tm, D), lambda i: (i, ga_b)),
                  pl.BlockSpec((tm, D), lambda i: (i, gb_b)),
                  pl.BlockSpec((tm, D), lambda i: (i, 0)),
                  pl.BlockSpec((DN_W, D), const),
                  pl.BlockSpec((DIL_W, D), const),
                  pl.BlockSpec((D, D), const),
                  pl.BlockSpec((1, D), const)],
        out_specs=pl.BlockSpec((tm, D), lambda i: (i, 0)),
        out_shape=jax.ShapeDtypeStruct((T, D), F32),
        compiler_params=pltpu.CompilerParams(dimension_semantics=("arbitrary",),
                                             vmem_limit_bytes=VMEM_LIMIT),
        name="out_merge",
    )(oa, ob, p2, p2, x2d, wa, wb, wo, fw)


def _split_cols(w):
    out, start = [], 0
    for s in PROJ_SIZES:
        out.append(w[:, start:start + s])
        start += s
    return out


def _layer(x, norm_w, w_in, conv_w, a_log, dt_bias, dn_norm_w, w_o_dn, w_o_dil, w_out, final_w, final):
    B, S, D = x.shape
    T = B * S
    (wq_a, wk_a, wv_a, wz_a, wb_a, wa_a, wq_b, wk_b, wv_b, wz_b, wg_a, wg_b) = _split_cols(w_in)
    w_small = jnp.concatenate([wb_a, wa_a, jnp.zeros((D, LANES - 2 * DN_HEADS), w_in.dtype)], axis=1).astype(BF16)
    w1 = jnp.concatenate([wq_a, wk_a, wv_a], axis=1).astype(BF16)

    def group_w(g):
        sl = slice(g * DIL_W, (g + 1) * DIL_W)
        return jnp.concatenate([wq_b[:, sl], wk_b[:, sl], wv_b[:, sl]], axis=1)

    w2 = jnp.concatenate([wz_a, wg_a, wg_b, wz_b, group_w(0)], axis=1).astype(BF16)
    ga_off = DN_W
    gb_off = ga_off + D
    zb_off = gb_off + D
    off0 = zb_off + DIL_W
    tn = D
    tiles2 = (((0, tn, "silu"),), ((0, tn, "sigmoid"),), ((0, tn, "sigmoid"),),
              ((0, DIL_W, "silu"), (DIL_W, tn, "none")), ((0, tn, "none"),))
    plain = (((0, GROUP_W, "none"),),)

    h, h4, h16, small = _norm_call(x, norm_w.astype(F32), w_small)
    qkv = _proj_conv_call(h.reshape(T, D), w1, conv_w, S, "proj_conv_deltanet")
    p2 = _proj_call(h.reshape(T, D), w2, tn, tiles2, BF16, "proj_gates_group0")
    g1 = _proj_call(h4.reshape(T, D), group_w(1).astype(BF16), GROUP_W, plain, BF16, "proj_group1")
    g2 = _proj_call(h16.reshape(T, D), group_w(2).astype(BF16), GROUP_W, plain, BF16, "proj_group2")

    o_a = _deltanet_call(qkv, small.reshape(T, LANES), p2, a_log, dt_bias, dn_norm_w, B, S)
    o_b = _attn_call(p2, off0, zb_off, g1, g2, B, S)
    out = _out_call(o_a, o_b.reshape(T, DIL_W), p2, ga_off, gb_off, x.reshape(T, D),
                    w_o_dn.astype(BF16), w_o_dil.astype(BF16), w_out.astype(BF16),
                    final_w.astype(F32).reshape(1, D), final)
    return out.reshape(B, S, D)


def kernel(x, norm_w, w_in, conv_w, a_log, dt_bias, dn_norm_w, w_o_dn, w_o_dil, w_out, final_norm_w):
    depth = norm_w.shape[0]
    for l in range(depth):
        x = _layer(x, norm_w[l], w_in[l], conv_w[l], a_log[l], dt_bias[l], dn_norm_w[l],
                   w_o_dn[l], w_o_dil[l], w_out[l], final_norm_w, final=(l == depth - 1))
    return x
```

```python
import functools
import math

import jax
import jax.numpy as jnp
import numpy as np
from jax import lax
from jax.experimental import pallas as pl
from jax.experimental.pallas import tpu as pltpu

F32 = jnp.float32
BF16 = jnp.bfloat16

D_MODEL = 1024
DN_HEADS = 8
DN_DK = 128
DN_DV = 128
DN_CONV = 4
DN_CHUNK = 64
DIL_GROUPS = ((128, 1), (512, 4), (2048, 16))
DIL_HEADS = 4
DIL_DH = 128
ATT_BLOCK = 128
NORM_EPS = 1e-6
LOG2E = math.log2(math.e)
N_DIL = len(DIL_GROUPS)
DN_W = DN_HEADS * DN_DK
DIL_W = DIL_HEADS * DIL_DH
GROUP_W = 3 * DIL_W
PROJ_SIZES = (DN_W, DN_W, DN_W, DN_W, DN_HEADS, DN_HEADS,
              N_DIL * DIL_W, N_DIL * DIL_W, N_DIL * DIL_W, DIL_W, D_MODEL, D_MODEL)

LANES = 128
VMEM_LIMIT = 56 * 1024 * 1024

NORM_TM = 512
PROJ_TM = 2048
PROJ_TN = 1024
PROJ_SUB = 512
CONV_SUB = 256
DN_TB = 512
DN_UNIT = 2
ATT_BATCH = 8
OUT_TM = 1024


def _sigmoid(x):
    return 1.0 / (1.0 + jnp.exp2(x * (-LOG2E)))


def _silu(x):
    return x * _sigmoid(x)


def _softplus(x):
    return jnp.maximum(x, 0.0) + jnp.log1p(jnp.exp(-jnp.abs(x)))


def _mm(a, b):
    return lax.dot_general(a.astype(BF16), b.astype(BF16), (((1,), (0,)), ((), ())),
                           preferred_element_type=F32)


def _mm_nt(a, b):
    return lax.dot_general(a.astype(BF16), b.astype(BF16), (((1,), (1,)), ((), ())),
                           preferred_element_type=F32)


def _mm_tn(a, b):
    return lax.dot_general(a.astype(BF16), b.astype(BF16), (((0,), (0,)), ((), ())),
                           preferred_element_type=F32)


def _norm_kernel(x_ref, w_ref, ws_ref, h_ref, h4_ref, h16_ref, sm_ref, slab_ref):
    x = x_ref[...]
    y = x * lax.rsqrt(jnp.mean(x * x, axis=-1, keepdims=True) + NORM_EPS) * w_ref[...]
    hb = y.astype(BF16)
    h_ref[...] = hb
    sm_ref[...] = jnp.dot(hb, ws_ref[...], preferred_element_type=F32)
    n_slabs = D_MODEL // LANES
    for c in range(n_slabs):
        slab_ref[c] = y[:, c * LANES:(c + 1) * LANES]
    tm = x.shape[0]
    for d, out in ((4, h4_ref), (16, h16_ref)):
        n = tm // d
        for r in range(d):
            for c in range(n_slabs):
                out[r, :, c * LANES:(c + 1) * LANES] = slab_ref[c, pl.ds(r, n, stride=d), :].astype(BF16)


def _norm_call(x, w, w_small):
    B, S, D = x.shape
    tm = NORM_TM
    nt = S // tm
    return pl.pallas_call(
        _norm_kernel,
        grid=(B, nt),
        in_specs=[pl.BlockSpec((None, tm, D), lambda b, i: (b, i, 0)),
                  pl.BlockSpec((1, D), lambda b, i: (0, 0)),
                  pl.BlockSpec((D, LANES), lambda b, i: (0, 0))],
        out_specs=[pl.BlockSpec((None, tm, D), lambda b, i: (b, i, 0)),
                   pl.BlockSpec((None, 4, tm // 4, D), lambda b, i: (b, 0, i, 0)),
                   pl.BlockSpec((None, 16, tm // 16, D), lambda b, i: (b, 0, i, 0)),
                   pl.BlockSpec((None, tm, LANES), lambda b, i: (b, i, 0))],
        out_shape=[jax.ShapeDtypeStruct((B, S, D), BF16),
                   jax.ShapeDtypeStruct((B, 4, S // 4, D), BF16),
                   jax.ShapeDtypeStruct((B, 16, S // 16, D), BF16),
                   jax.ShapeDtypeStruct((B, S, LANES), F32)],
        scratch_shapes=[pltpu.VMEM((D // LANES, tm, LANES), F32)],
        compiler_params=pltpu.CompilerParams(dimension_semantics=("arbitrary", "arbitrary"),
                                             vmem_limit_bytes=VMEM_LIMIT),
        name="norm_deinterleave",
    )(x, w.reshape(1, D), w_small)


_ACTS = {"none": lambda v: v, "silu": _silu, "sigmoid": _sigmoid}


def _proj_kernel(h_ref, w_ref, o_ref, *, tiles):
    tm = h_ref.shape[0]
    sub = PROJ_SUB

    def run(parts):
        for m in range(tm // sub):
            rows = slice(m * sub, (m + 1) * sub)
            acc = jnp.dot(h_ref[rows, :], w_ref[...], preferred_element_type=F32)
            for lo, hi, act in parts:
                o_ref[rows, lo:hi] = _ACTS[act](acc[:, lo:hi]).astype(o_ref.dtype)

    kinds = sorted(set(tiles), key=tiles.index)
    if len(kinds) == 1:
        run(kinds[0])
        return
    j = pl.program_id(1)
    for parts in kinds:
        cond = functools.reduce(lambda a, b: a | b, [j == t for t, p in enumerate(tiles) if p == parts])
        pl.when(cond)(functools.partial(run, parts))


def _proj_call(h2d, w, tn, tiles, out_dtype, name):
    T, D = h2d.shape
    N = w.shape[1]
    tm = PROJ_TM
    assert len(tiles) * tn == N
    return pl.pallas_call(
        functools.partial(_proj_kernel, tiles=tiles),
        grid=(T // tm, N // tn),
        in_specs=[pl.BlockSpec((tm, D), lambda i, j: (i, 0)),
                  pl.BlockSpec((D, tn), lambda i, j: (0, j))],
        out_specs=pl.BlockSpec((tm, tn), lambda i, j: (i, j)),
        out_shape=jax.ShapeDtypeStruct((T, N), out_dtype),
        compiler_params=pltpu.CompilerParams(dimension_semantics=("arbitrary", "arbitrary"),
                                             vmem_limit_bytes=VMEM_LIMIT),
        name=name,
    )(h2d, w)


def _proj_conv_kernel(h_ref, w_ref, cw_ref, o_ref, tail_ref, *, tiles_per_seq):
    i = pl.program_id(1)
    tm = h_ref.shape[0]
    sub = CONV_SUB

    @pl.when(i % tiles_per_seq == 0)
    def _():
        tail_ref[...] = jnp.zeros_like(tail_ref)

    assert DN_CONV == 4
    prev = tail_ref[...]
    taps = [cw_ref[k:k + 1, :] for k in range(DN_CONV)]
    for m in range(tm // sub):
        acc = jnp.dot(h_ref[m * sub:(m + 1) * sub, :], w_ref[...], preferred_element_type=F32)
        ext = jnp.concatenate([prev, acc], axis=0)
        ext1 = pltpu.roll(ext, 1, axis=0)
        b = taps[1] * ext + taps[0] * ext1
        y = taps[3] * acc + taps[2] * ext1[8:, :] + pltpu.roll(b, 2, axis=0)[8:, :]
        o_ref[m * sub:(m + 1) * sub, :] = _silu(y).astype(o_ref.dtype)
        prev = acc[sub - 8:sub, :]
    tail_ref[...] = prev


def _proj_conv_call(h2d, w, conv_w, S, name):
    T, D = h2d.shape
    N = w.shape[1]
    tm, tn = PROJ_TM, PROJ_TN
    assert N % tn == 0 and T % tm == 0 and S % tm == 0
    cw = jnp.zeros((8, N), F32).at[0:DN_CONV].set(conv_w.astype(F32))
    return pl.pallas_call(
        functools.partial(_proj_conv_kernel, tiles_per_seq=S // tm),
        grid=(N // tn, T // tm),
        in_specs=[pl.BlockSpec((tm, D), lambda j, i: (i, 0)),
                  pl.BlockSpec((D, tn), lambda j, i: (0, j)),
                  pl.BlockSpec((8, tn), lambda j, i: (0, j))],
        out_specs=pl.BlockSpec((tm, tn), lambda j, i: (i, j)),
        out_shape=jax.ShapeDtypeStruct((T, N), BF16),
        scratch_shapes=[pltpu.VMEM((8, tn), F32)],
        compiler_params=pltpu.CompilerParams(dimension_semantics=("arbitrary", "arbitrary"),
                                             vmem_limit_bytes=VMEM_LIMIT),
        name=name,
    )(h2d, w, cw)


def _deltanet_kernel(qkv_ref, sm_ref, z_ref, hp_ref, nw_ref, o_ref, st_ref):
    C = DN_CHUNK
    tb = o_ref.shape[0]

    @pl.when(pl.program_id(1) == 0)
    def _():
        st_ref[...] = jnp.zeros_like(st_ref)

    row = lax.broadcasted_iota(jnp.int32, (C, C), 0)
    col = lax.broadcasted_iota(jnp.int32, (C, C), 1)
    causal = row >= col
    strict = row > col
    eye =(row == col).astype(F32)
    hp = hp_ref[...]
    neg_a = -jnp.exp(hp[0:1, :])
    dt_b = hp[1:2, :]
    nw = nw_ref[...]

    heads = range(DN_HEADS)

    def bcast(a, lane):
        return jnp.broadcast_to(a[:, lane:lane + 1], (C, LANES))

    def load(base, c0):
        return qkv_ref[base:base + C, c0:c0 + LANES].astype(F32)

    row_l = lax.broadcasted_iota(jnp.int32, (C, LANES), 0)

    def cumsum_rows(v):
        shift = 1
        while shift < C:
            v = v + jnp.where(row_l >= shift, pltpu.roll(v, shift, axis=0), 0.0)
            shift *= 2
        return v

    def prologue(unit, out):
        dec = {}
        for c in unit:
            sm = sm_ref[c * C:(c + 1) * C, :]
            g_all = neg_a * _softplus(sm + dt_b)
            gc_all = cumsum_rows(g_all)
            gc_t = jnp.concatenate([gc_all, jnp.zeros_like(gc_all)], axis=0).T
            glast_all = gc_all[C - 1:C, :]
            dec[c] = dict(beta=_sigmoid(sm), gc=gc_all, gc_t=gc_t, eg=jnp.exp(gc_all),
                          ek=jnp.exp(glast_all - gc_all), dl=jnp.exp(glast_all))
            yield
        q, k, kb, eg, gamma, rhs = [], [], [], [], [], []
        for c in unit:
            for h in heads:
                lane = DN_HEADS + h
                base = c * C
                qh = load(base, h * DN_DK)
                kh = load(base, DN_W + h * DN_DK)
                vh = load(base, 2 * DN_W + h * DN_DV)
                qh = qh * (lax.rsqrt(jnp.sum(qh * qh, axis=-1, keepdims=True) + NORM_EPS) * (DN_DK ** -0.5))
                kh = kh * lax.rsqrt(jnp.sum(kh * kh, axis=-1, keepdims=True) + NORM_EPS)
                beta = bcast(dec[c]["beta"], h)
                egh = bcast(dec[c]["eg"], lane)
                kbh = kh * beta
                q.append(qh)
                k.append(kh)
                kb.append(kbh)
                eg.append(egh)
                rhs.append(jnp.concatenate([vh * beta, kbh * egh], axis=1))
                diff = bcast(dec[c]["gc"], lane)[:, 0:C] - dec[c]["gc_t"][lane:lane + 1, 0:C]
                gamma.append(jnp.exp(jnp.where(causal, diff, -jnp.inf)))
                if h % 2 == 1:
                    yield
        out.update(dec=dec, q=q, k=k, kb=kb, eg=eg, gamma=gamma, rhs=rhs)

    def solve(unit, pro, out):
        idx = range(len(unit) * DN_HEADS)
        q, k, kb, gamma, rhs = pro["q"], pro["k"], pro["kb"], pro["gamma"], pro["rhs"]
        a = [_mm_nt(jnp.concatenate([kb[i], q[i]], axis=0), k[i]) for i in idx]
        yield
        p = [jnp.where(strict, -(a[i][0:C] * gamma[i]), 0.0) for i in idx]
        aqk = [a[i][C:] * gamma[i] for i in idx]
        t = [eye + p[i] for i in idx]
        p = [_mm(p[i], p[i]) for i in idx]
        yield
        for _ in range(4):
            r = [_mm(jnp.concatenate([p[i], t[i]], axis=0), p[i]) for i in idx]
            t = [t[i] + r[i][C:] for i in idx]
            p = [r[i][0:C] for i in idx]
            yield
        t = [t[i] + _mm(t[i], p[i]) for i in idx]
        yield
        out.update(x=[_mm(t[i], rhs[i]) for i in idx], aqk=aqk)

    state = [st_ref[h] for h in heads]

    def recur(unit, pro, sol):
        for n, c in enumerate(unit):
            base = c * C
            at = lambda lst, h: lst[n * DN_HEADS + h]
            ws_qs = [_mm(jnp.concatenate([at(sol["x"], h)[:, DN_DV:], at(pro["q"], h) * at(pro["eg"], h)], axis=0),
                         state[h]) for h in heads]
            yield
            v_new = [at(sol["x"], h)[:, 0:DN_DV] - ws_qs[h][0:C] for h in heads]
            o = [ws_qs[h][C:] + _mm(at(sol["aqk"], h), v_new[h]) for h in heads]
            for h in heads:
                lane = DN_HEADS + h
                kd = at(pro["k"], h) * bcast(pro["dec"][c]["ek"], lane)
                dl = jnp.broadcast_to(pro["dec"][c]["dl"][:, lane:lane + 1], (DN_DK, DN_DV))
                state[h] = state[h] * dl + _mm_tn(kd, v_new[h])
            yield
            for h in heads:
                y = o[h] * lax.rsqrt(jnp.mean(o[h] * o[h], axis=-1, keepdims=True) + NORM_EPS) * nw
                gate = z_ref[base:base + C, h * DN_DV:(h + 1) * DN_DV].astype(F32)
                o_ref[base:base + C, h * DN_DV:(h + 1) * DN_DV] = (y * gate).astype(o_ref.dtype)
                if h % 4 == 3:
                    yield

    def interleave(gens):
        gens = list(gens)
        while gens:
            for g in list(gens):
                try:
                    next(g)
                except StopIteration:
                    gens.remove(g)

    n_chunks = tb // C
    units = [list(range(u, min(u + DN_UNIT, n_chunks))) for u in range(0, n_chunks, DN_UNIT)]
    pro = [dict() for _ in units]
    sol = [dict() for _ in units]
    for tick in range(len(units) + 2):
        active = []
        if 0 <= tick - 2 < len(units):
            active.append(recur(units[tick - 2], pro[tick - 2], sol[tick - 2]))
        if 0 <= tick - 1 < len(units):
            active.append(solve(units[tick - 1], pro[tick - 1], sol[tick - 1]))
        if tick < len(units):
            active.append(prologue(units[tick], pro[tick]))
        interleave(active)
    for h in heads:
        st_ref[h] = state[h]


def _deltanet_call(qkv, small, p2, a_log, dt_bias, dn_norm_w, B, S):
    tb = DN_TB
    nt = S // tb
    hp = jnp.zeros((8, LANES), F32)
    hp = hp.at[0, DN_HEADS:2 * DN_HEADS].set(a_log.astype(F32)).at[1, DN_HEADS:2 * DN_HEADS].set(dt_bias.astype(F32))
    nw = dn_norm_w.astype(F32).reshape(1, DN_DV)
    return pl.pallas_call(
        _deltanet_kernel,
        grid=(B, nt),
        in_specs=[pl.BlockSpec((tb, 3 * DN_W), lambda b, i: (b * nt + i, 0)),
                  pl.BlockSpec((tb, LANES), lambda b, i: (b * nt + i, 0)),
                  pl.BlockSpec((tb, DN_W), lambda b, i: (b * nt + i, 0)),
                  pl.BlockSpec((8, LANES), lambda b, i: (0, 0)),
                  pl.BlockSpec((1, DN_DV), lambda b, i: (0, 0))],
        out_specs=pl.BlockSpec((tb, DN_W), lambda b, i: (b * nt + i, 0)),
        out_shape=jax.ShapeDtypeStruct((B * S, DN_W), BF16),
        scratch_shapes=[pltpu.VMEM((DN_HEADS, DN_DK, DN_DV), F32)],
        compiler_params=pltpu.CompilerParams(dimension_semantics=("arbitrary", "arbitrary"),
                                             vmem_limit_bytes=VMEM_LIMIT),
        name="deltanet",
    )(qkv, small, p2, hp, nw)


def _alibi_slope(idx):
    n = N_DIL * DIL_HEADS
    return 2.0 ** (-8.0 * (idx + 1) / n)


def _attn_kernel(q0_ref, k0_ref, v0_ref, q1_ref, k1_ref, v1_ref, q2_ref, k2_ref, v2_ref, z_ref, o_ref,
                 num_ref, den_ref, mx_ref):
    Q = ATT_BLOCK
    hd = pl.program_id(1)
    span = DIL_GROUPS[0][0] // DIL_GROUPS[0][1]
    dist = (Q + lax.broadcasted_iota(jnp.int32, (Q, 2 * Q), 0) - lax.broadcasted_iota(jnp.int32, (Q, 2 * Q), 1))
    valid = (dist >= 0) & (dist <= span)
    distf = dist.astype(F32)

    def slope_of(g):
        s = jnp.float32(_alibi_slope(g * DIL_HEADS))
        for h in range(1, DIL_HEADS):
            s = jnp.where(hd == h, jnp.float32(_alibi_slope(g * DIL_HEADS + h)), s)
        return s

    def group_batches(g, q_ref, k_ref, v_ref, emit):
        d = DIL_GROUPS[g][1]
        bias2 = jnp.where(valid, -(slope_of(g) * (float(d) * LOG2E)) * distf, -jnp.inf)
        bias1 = bias2[:, Q:]
        nb = q_ref.shape[-2] // Q

        def item(r, j):
            qr, kr, vr = (q_ref, k_ref, v_ref) if d == 1 else (q_ref.at[r], k_ref.at[r], v_ref.at[r])
            if j == 0:
                return qr[0:Q, :], kr[0:Q, :], vr[0:Q, :], bias1
            lo, cur = (j - 1) * Q, j * Q
            return qr[cur:cur + Q, :], kr[lo:lo + 2 * Q, :], vr[lo:lo + 2 * Q, :], bias2

        def batch(ids):
            items = [item(r, j) for r, j in ids]
            s = [_mm_nt(q, k) + bias for q, k, v, bias in items]
            mx = [jnp.max(si, axis=-1, keepdims=True) for si in s]
            yield
            p = [jnp.exp2(si - mi) for si, mi in zip(s, mx)]
            den = [jnp.sum(pi, axis=-1, keepdims=True) for pi in p]
            num = [_mm(pi, it[2]) for pi, it in zip(p, items)]
            yield
            for (r, j), n, dn, m in zip(ids, num, den, mx):
                emit(r, j, (n, jnp.broadcast_to(dn, (Q, LANES)), jnp.broadcast_to(m, (Q, LANES))))

        ids = [(r, j) for r in range(d) for j in range(nb)]
        return [batch(ids[i:i + ATT_BATCH]) for i in range(0, len(ids), ATT_BATCH)]

    def rows(g, r, j):
        d = DIL_GROUPS[g][1]
        if d == 1:
            return pl.ds(j * Q, Q)
        return pl.ds(j * (Q * d) + r, Q, stride=d)

    def store_to(g, slot):
        def emit(r, j, res):
            idx = rows(g, r, j)
            num_ref[slot, idx, :] = res[0]
            den_ref[slot, idx, :] = res[1]
            mx_ref[slot, idx, :] = res[2]
        return emit

    def emit_last(r, j, res):
        idx = rows(0, r, j)
        parts = [(num_ref[s, idx, :], den_ref[s, idx, :], mx_ref[s, idx, :]) for s in range(2)] + [res]
        m = jnp.maximum(jnp.maximum(parts[0][2], parts[1][2]), parts[2][2])
        w = [jnp.exp2(p[2] - m) for p in parts]
        n = parts[0][0] * w[0] + parts[1][0] * w[1] + parts[2][0] * w[2]
        dn = parts[0][1] * w[0] + parts[1][1] * w[1] + parts[2][1] * w[2]
        o_ref[idx, :] = ((n / dn) * z_ref[idx, :].astype(F32)).astype(o_ref.dtype)

    pending = (group_batches(2, q2_ref, k2_ref, v2_ref, store_to(2, 0))
               + group_batches(1, q1_ref, k1_ref, v1_ref, store_to(1, 1))
               + group_batches(0, q0_ref, k0_ref, v0_ref, emit_last))
    active = []
    while pending or active:
        if pending:
            active.append(pending.pop(0))
        for gen in list(active):
            try:
                next(gen)
            except StopIteration:
                active.remove(gen)


def _attn_call(p2, off0, zb_off, g1, g2, B, S):
    H = DIL_HEADS
    p2v = p2.reshape(B, S, p2.shape[1])
    g1v = g1.reshape(B, 4, S // 4, GROUP_W)
    g2v = g2.reshape(B, 16, S // 16, GROUP_W)
    b0 = off0 // LANES
    zb0 = zb_off // LANES

    def nat(cb):
        return pl.BlockSpec((None, S, LANES), lambda b, h: (b, 0, cb + h))

    def perm(d, cb):
        return pl.BlockSpec((None, d, S // d, LANES), lambda b, h: (b, 0, 0, cb + h))

    return pl.pallas_call(
        _attn_kernel,
        grid=(B, H),
        in_specs=[nat(b0), nat(b0 + H), nat(b0 + 2 * H),
                  perm(4, 0), perm(4, H), perm(4, 2 * H),
                  perm(16, 0), perm(16, H), perm(16, 2 * H),
                  nat(zb0)],
        out_specs=pl.BlockSpec((None, S, LANES), lambda b, h: (b, 0, h)),
        out_shape=jax.ShapeDtypeStruct((B, S, DIL_W), BF16),
        scratch_shapes=[pltpu.VMEM((2, S, LANES), F32)] * 3,
        compiler_params=pltpu.CompilerParams(dimension_semantics=("arbitrary", "arbitrary"),
                                             vmem_limit_bytes=VMEM_LIMIT),
        name="dilated_attention",
    )(p2v, p2v, p2v, g1v, g1v, g1v, g2v, g2v, g2v, p2v)


def _out_kernel(oa_ref, ob_ref, ga_ref, gb_ref, x_ref, wa_ref, wb_ref, wo_ref, fw_ref, o_ref, *, final):
    ya = jnp.dot(oa_ref[...], wa_ref[...], preferred_element_type=F32)
    yb = jnp.dot(ob_ref[...], wb_ref[...], preferred_element_type=F32)
    merged = ga_ref[...].astype(F32) * ya + gb_ref[...].astype(F32) * yb
    xn = x_ref[...] + jnp.dot(merged.astype(BF16), wo_ref[...], preferred_element_type=F32)
    if final:
        xn = xn * lax.rsqrt(jnp.mean(xn * xn, axis=-1, keepdims=True) + NORM_EPS) * fw_ref[...]
    o_ref[...] = xn


def _out_call(oa, ob, p2, ga_off, gb_off, x2d, wa, wb, wo, fw, final):
    T, D = x2d.shape
    tm = OUT_TM
    ga_b = ga_off // D
    gb_b = gb_off // D
    const = lambda i: (0, 0)
    return pl.pallas_call(
        functools.partial(_out_kernel, final=final),
        grid=(T // tm,),
        in_specs=[pl.BlockSpec((tm, DN_W), lambda i: (i, 0)),
                  pl.BlockSpec((tm, DIL_W), lambda i: (i, 0)),
                  pl.BlockSpec((tm, D), lambda i: (i, ga_b)),
                  pl.BlockSpec((tm, D), lambda i: (i, gb_b)),
                  pl.BlockSpec((tm, D), lambda i: (i, 0)),
                  pl.BlockSpec((DN_W, D), const),
                  pl.BlockSpec((DIL_W, D), const),
                  pl.BlockSpec((D, D), const),
                  pl.BlockSpec((1, D), const)],
        out_specs=pl.BlockSpec((tm, D), lambda i: (i, 0)),
        out_shape=jax.ShapeDtypeStruct((T, D), F32),
        compiler_params=pltpu.CompilerParams(dimension_semantics=("arbitrary",),
                                             vmem_limit_bytes=VMEM_LIMIT),
        name="out_merge",
    )(oa, ob, p2, p2, x2d, wa, wb, wo, fw)


def _split_cols(w):
    out, start = [], 0
    for s in PROJ_SIZES:
        out.append(w[:, start:start + s])
        start += s
    return out


def _layer(x, norm_w, w_in, conv_w, a_log, dt_bias, dn_norm_w, w_o_dn, w_o_dil, w_out, final_w, final):
    B, S, D = x.shape
    T = B * S
    (wq_a, wk_a, wv_a, wz_a, wb_a, wa_a, wq_b, wk_b, wv_b, wz_b, wg_a, wg_b) = _split_cols(w_in)
    w_small = jnp.concatenate([wb_a, wa_a, jnp.zeros((D, LANES - 2 * DN_HEADS), w_in.dtype)], axis=1).astype(BF16)
    w1 = jnp.concatenate([wq_a, wk_a, wv_a], axis=1).astype(BF16)

    def group_w(g):
        sl = slice(g * DIL_W, (g + 1) * DIL_W)
        wq = wq_b[:, sl] * (DIL_DH ** -0.5 * LOG2E)
        return jnp.concatenate([wq, wk_b[:, sl], wv_b[:, sl]], axis=1)

    w2 = jnp.concatenate([wz_a, wg_a, wg_b, wz_b, group_w(0)], axis=1).astype(BF16)
    ga_off = DN_W
    gb_off = ga_off + D
    zb_off = gb_off + D
    off0 = zb_off + DIL_W
    tn = D
    tiles2 = (((0, tn, "silu"),), ((0, tn, "sigmoid"),), ((0, tn, "sigmoid"),),
              ((0, DIL_W, "silu"), (DIL_W, tn, "none")), ((0, tn, "none"),))
    plain = (((0, GROUP_W, "none"),),)

    h, h4, h16, small = _norm_call(x, norm_w.astype(F32), w_small)
    qkv = _proj_conv_call(h.reshape(T, D), w1, conv_w, S, "proj_conv_deltanet")
    p2 = _proj_call(h.reshape(T, D), w2, tn, tiles2, BF16, "proj_gates_group0")
    g1 = _proj_call(h4.reshape(T, D), group_w(1).astype(BF16), GROUP_W, plain, BF16, "proj_group1")
    g2 = _proj_call(h16.reshape(T, D), group_w(2).astype(BF16), GROUP_W, plain, BF16, "proj_group2")

    o_a = _deltanet_call(qkv, small.reshape(T, LANES), p2, a_log, dt_bias, dn_norm_w, B, S)
    o_b = _attn_call(p2, off0, zb_off, g1, g2, B, S)
    out = _out_call(o_a, o_b.reshape(T, DIL_W), p2, ga_off, gb_off, x.reshape(T, D),
                    w_o_dn.astype(BF16), w_o_dil.astype(BF16), w_out.astype(BF16),
                    final_w.astype(F32).reshape(1, D), final)
    return out.reshape(B, S, D)


def kernel(x, norm_w, w_in, conv_w, a_log, dt_bias, dn_norm_w, w_o_dn, w_o_dil, w_out, final_norm_w):
    depth = norm_w.shape[0]
    for l in range(depth):
        x = _layer(x, norm_w[l], w_in[l], conv_w[l], a_log[l], dt_bias[l], dn_norm_w[l],
                   w_o_dn[l], w_o_dil[l], w_out[l], final_norm_w, final=(l == depth - 1))
    return x
```

```python
import functools
import math

import jax
import jax.numpy as jnp
import numpy as np
from jax import lax
from jax.experimental import pallas as pl
from jax.experimental.pallas import tpu as pltpu

F32 = jnp.float32
BF16 = jnp.bfloat16

D_MODEL = 1024
DN_HEADS = 8
DN_DK = 128
DN_DV = 128
DN_CONV = 4
DN_CHUNK = 64
DIL_GROUPS = ((128, 1), (512, 4), (2048, 16))
DIL_HEADS = 4
DIL_DH = 128
ATT_BLOCK = 128
NORM_EPS = 1e-6
LOG2E = math.log2(math.e)
N_DIL = len(DIL_GROUPS)
DN_W = DN_HEADS * DN_DK
DIL_W = DIL_HEADS * DIL_DH
GROUP_W = 3 * DIL_W
PROJ_SIZES = (DN_W, DN_W, DN_W, DN_W, DN_HEADS, DN_HEADS,
              N_DIL * DIL_W, N_DIL * DIL_W, N_DIL * DIL_W, DIL_W, D_MODEL, D_MODEL)

LANES = 128
VMEM_LIMIT = 56 * 1024 * 1024

NORM_TM = 512
PROJ_TM = 2048
PROJ_TN = 1024
PROJ_SUB = 512
CONV_SUB = 256
DN_TB = 512
DN_UNIT = 2
ATT_BATCH = 8
OUT_TM = 1024


def _sigmoid(x):
    return 1.0 / (1.0 + jnp.exp2(x * (-LOG2E)))


def _silu(x):
    return x * _sigmoid(x)


def _softplus(x):
    return jnp.maximum(x, 0.0) + jnp.log1p(jnp.exp(-jnp.abs(x)))


def _mm(a, b):
    return lax.dot_general(a.astype(BF16), b.astype(BF16), (((1,), (0,)), ((), ())),
                           preferred_element_type=F32)


def _mm_nt(a, b):
    return lax.dot_general(a.astype(BF16), b.astype(BF16), (((1,), (1,)), ((), ())),
                           preferred_element_type=F32)


def _mm_tn(a, b):
    return lax.dot_general(a.astype(BF16), b.astype(BF16), (((0,), (0,)), ((), ())),
                           preferred_element_type=F32)


def _norm_kernel(x_ref, w_ref, ws_ref, h_ref, h4_ref, h16_ref, sm_ref, slab_ref):
    x = x_ref[...]
    y = x * lax.rsqrt(jnp.mean(x * x, axis=-1, keepdims=True) + NORM_EPS) * w_ref[...]
    hb = y.astype(BF16)
    h_ref[...] = hb
    sm_ref[...] = jnp.dot(hb, ws_ref[...], preferred_element_type=F32)
    n_slabs = D_MODEL // LANES
    for c in range(n_slabs):
        slab_ref[c] = y[:, c * LANES:(c + 1) * LANES]
    tm = x.shape[0]
    for d, out in ((4, h4_ref), (16, h16_ref)):
        n = tm // d
        for r in range(d):
            for c in range(n_slabs):
                out[r, :, c * LANES:(c + 1) * LANES] = slab_ref[c, pl.ds(r, n, stride=d), :].astype(BF16)


def _norm_call(x, w, w_all, small_col):
    B, S, D = x.shape
    tm = NORM_TM
    nt = S // tm
    small_blk = small_col // LANES
    return pl.pallas_call(
        _norm_kernel,
        grid=(B, nt),
        in_specs=[pl.BlockSpec((None, tm, D), lambda b, i: (b, i, 0)),
                  pl.BlockSpec((1, D), lambda b, i: (0, 0)),
                  pl.BlockSpec((D, LANES), lambda b, i: (0, small_blk))],
        out_specs=[pl.BlockSpec((None, tm, D), lambda b, i: (b, i, 0)),
                   pl.BlockSpec((None, 4, tm // 4, D), lambda b, i: (b, 0, i, 0)),
                   pl.BlockSpec((None, 16, tm // 16, D), lambda b, i: (b, 0, i, 0)),
                   pl.BlockSpec((None, tm, LANES), lambda b, i: (b, i, 0))],
        out_shape=[jax.ShapeDtypeStruct((B, S, D), BF16),
                   jax.ShapeDtypeStruct((B, 4, S // 4, D), BF16),
                   jax.ShapeDtypeStruct((B, 16, S // 16, D), BF16),
                   jax.ShapeDtypeStruct((B, S, LANES), F32)],
        scratch_shapes=[pltpu.VMEM((D // LANES, tm, LANES), F32)],
        compiler_params=pltpu.CompilerParams(dimension_semantics=("arbitrary", "arbitrary"),
                                             vmem_limit_bytes=VMEM_LIMIT),
        name="norm_deinterleave",
    )(x, w.reshape(1, D), w_all)


_ACTS = {"none": lambda v: v, "silu": _silu, "sigmoid": _sigmoid}


def _proj_kernel(h_ref, w_ref, o_ref, *, tiles):
    tm = h_ref.shape[0]
    sub = PROJ_SUB

    def run(parts):
        for m in range(tm // sub):
            rows = slice(m * sub, (m + 1) * sub)
            acc = jnp.dot(h_ref[rows, :], w_ref[...], preferred_element_type=F32)
            for lo, hi, act in parts:
                o_ref[rows, lo:hi] = _ACTS[act](acc[:, lo:hi]).astype(o_ref.dtype)

    kinds = sorted(set(tiles), key=tiles.index)
    if len(kinds) == 1:
        run(kinds[0])
        return
    j = pl.program_id(1)
    for parts in kinds:
        cond = functools.reduce(lambda a, b: a | b, [j == t for t, p in enumerate(tiles) if p == parts])
        pl.when(cond)(functools.partial(run, parts))


def _proj_call(h2d, w_all, col0, tn, tiles, out_dtype, name):
    T, D = h2d.shape
    N = len(tiles) * tn
    tm = PROJ_TM
    assert col0 % tn == 0 and T % tm == 0
    blk0 = col0 // tn
    return pl.pallas_call(
        functools.partial(_proj_kernel, tiles=tiles),
        grid=(T // tm, N // tn),
        in_specs=[pl.BlockSpec((tm, D), lambda i, j: (i, 0)),
                  pl.BlockSpec((D, tn), lambda i, j: (0, blk0 + j))],
        out_specs=pl.BlockSpec((tm, tn), lambda i, j: (i, j)),
        out_shape=jax.ShapeDtypeStruct((T, N), out_dtype),
        compiler_params=pltpu.CompilerParams(dimension_semantics=("arbitrary", "arbitrary"),
                                             vmem_limit_bytes=VMEM_LIMIT),
        name=name,
    )(h2d, w_all)


def _proj_conv_kernel(h_ref, w_ref, cw_ref, o_ref, u_ref, y_ref, *, tiles_per_seq):
    i = pl.program_id(1)
    tm = h_ref.shape[0]
    sub = CONV_SUB
    half = sub // 2
    n_slab = u_ref.shape[0]

    @pl.when(i % tiles_per_seq == 0)
    def _():
        u_ref[:, 0:8, :] = jnp.zeros((n_slab, 8, LANES), F32)

    @pl.when(i % tiles_per_seq != 0)
    def _():
        u_ref[:, 0:8, :] = u_ref[:, tm:tm + 8, :]

    for m in range(tm // sub):
        base = 8 + m * sub
        acc = jnp.dot(h_ref[m * sub:(m + 1) * sub, :], w_ref[...], preferred_element_type=F32)
        for c in range(n_slab):
            u_ref[c, base:base + sub, :] = acc[:, c * LANES:(c + 1) * LANES]
        for c in range(n_slab):
            taps = [cw_ref[k:k + 1, c * LANES:(c + 1) * LANES] for k in range(DN_CONV)]
            for phase in range(2):
                first = base + phase - (DN_CONV - 1)
                y = taps[0] * u_ref[c, pl.ds(first, half, stride=2), :]
                for k in range(1, DN_CONV):
                    y = y + taps[k] * u_ref[c, pl.ds(first + k, half, stride=2), :]
                y_ref[c, pl.ds(phase, half, stride=2), :] = y + y * jnp.tanh(y)
            o_ref[m * sub:(m + 1) * sub, c * LANES:(c + 1) * LANES] = y_ref[c].astype(o_ref.dtype)


def _proj_conv_call(h2d, w_all, col0, N, conv_w, S, name):
    T, D = h2d.shape
    tm, tn = PROJ_TM, PROJ_TN
    assert N % tn == 0 and col0 % tn == 0 and T % tm == 0 and S % tm == 0
    blk0 = col0 // tn
    cw = jnp.zeros((8, N), F32).at[0:DN_CONV].set(0.5 * conv_w.astype(F32))
    return pl.pallas_call(
        functools.partial(_proj_conv_kernel, tiles_per_seq=S // tm),
        grid=(N // tn, T // tm),
        in_specs=[pl.BlockSpec((tm, D), lambda j, i: (i, 0)),
                  pl.BlockSpec((D, tn), lambda j, i: (0, blk0 + j)),
                  pl.BlockSpec((8, tn), lambda j, i: (0, j))],
        out_specs=pl.BlockSpec((tm, tn), lambda j, i: (i, j)),
        out_shape=jax.ShapeDtypeStruct((T, N), BF16),
        scratch_shapes=[pltpu.VMEM((tn // LANES, tm + 8, LANES), F32),
                        pltpu.VMEM((tn // LANES, CONV_SUB, LANES), F32)],
        compiler_params=pltpu.CompilerParams(dimension_semantics=("arbitrary", "arbitrary"),
                                             vmem_limit_bytes=VMEM_LIMIT),
        name=name,
    )(h2d, w_all, cw)


def _deltanet_kernel(qkv_ref, sm_ref, z_ref, hp_ref, nw_ref, o_ref, st_ref):
    C = DN_CHUNK
    tb = o_ref.shape[0]

    @pl.when(pl.program_id(1) == 0)
    def _():
        st_ref[...] = jnp.zeros_like(st_ref)

    row = lax.broadcasted_iota(jnp.int32, (C, C), 0)
    col = lax.broadcasted_iota(jnp.int32, (C, C), 1)
    causal = row >= col
    strict = row > col
    eye =(row == col).astype(F32)
    hp = hp_ref[...]
    neg_a = -jnp.exp(hp[0:1, :])
    dt_b = hp[1:2, :]
    nw = nw_ref[...]

    heads = range(DN_HEADS)

    def bcast(a, lane):
        return jnp.broadcast_to(a[:, lane:lane + 1], (C, LANES))

    def load(base, c0):
        return qkv_ref[base:base + C, c0:c0 + LANES].astype(F32)

    row_l = lax.broadcasted_iota(jnp.int32, (C, LANES), 0)

    def cumsum_rows(v):
        shift = 1
        while shift < C:
            v = v + jnp.where(row_l >= shift, pltpu.roll(v, shift, axis=0), 0.0)
            shift *= 2
        return v

    def prologue(unit, out):
        dec = {}
        for c in unit:
            sm = sm_ref[c * C:(c + 1) * C, :]
            g_all = neg_a * _softplus(sm + dt_b)
            gc_all = cumsum_rows(g_all)
            gc_t = jnp.concatenate([gc_all, jnp.zeros_like(gc_all)], axis=0).T
            glast_all = gc_all[C - 1:C, :]
            dec[c] = dict(beta=_sigmoid(sm), gc=gc_all, gc_t=gc_t, eg=jnp.exp(gc_all),
                          ek=jnp.exp(glast_all - gc_all), dl=jnp.exp(glast_all))
            yield
        q, k, kb, eg, gamma, rhs = [], [], [], [], [], []
        for c in unit:
            for h in heads:
                lane = DN_HEADS + h
                base = c * C
                qh = load(base, h * DN_DK)
                kh = load(base, DN_W + h * DN_DK)
                vh = load(base, 2 * DN_W + h * DN_DV)
                qh = qh * (lax.rsqrt(jnp.sum(qh * qh, axis=-1, keepdims=True) + NORM_EPS) * (DN_DK ** -0.5))
                kh = kh * lax.rsqrt(jnp.sum(kh * kh, axis=-1, keepdims=True) + NORM_EPS)
                beta = bcast(dec[c]["beta"], h)
                egh = bcast(dec[c]["eg"], lane)
                kbh = kh * beta
                q.append(qh)
                k.append(kh)
                kb.append(kbh)
                eg.append(egh)
                rhs.append(jnp.concatenate([vh * beta, kbh * egh], axis=1))
                diff = bcast(dec[c]["gc"], lane)[:, 0:C] - dec[c]["gc_t"][lane:lane + 1, 0:C]
                gamma.append(jnp.exp(jnp.where(causal, diff, -jnp.inf)))
                if h % 2 == 1:
                    yield
        out.update(dec=dec, q=q, k=k, kb=kb, eg=eg, gamma=gamma, rhs=rhs)

    def solve(unit, pro, out):
        idx = range(len(unit) * DN_HEADS)
        q, k, kb, gamma, rhs = pro["q"], pro["k"], pro["kb"], pro["gamma"], pro["rhs"]
        a = [_mm_nt(jnp.concatenate([kb[i], q[i]], axis=0), k[i]) for i in idx]
        yield
        p = [jnp.where(strict, -(a[i][0:C] * gamma[i]), 0.0) for i in idx]
        aqk = [a[i][C:] * gamma[i] for i in idx]
        t = [eye + p[i] for i in idx]
        p = [_mm(p[i], p[i]) for i in idx]
        yield
        for _ in range(4):
            r = [_mm(jnp.concatenate([p[i], t[i]], axis=0), p[i]) for i in idx]
            t = [t[i] + r[i][C:] for i in idx]
            p = [r[i][0:C] for i in idx]
            yield
        t = [t[i] + _mm(t[i], p[i]) for i in idx]
        yield
        out.update(x=[_mm(t[i], rhs[i]) for i in idx], aqk=aqk)

    state = [st_ref[h] for h in heads]

    def recur(unit, pro, sol):
        for n, c in enumerate(unit):
            base = c * C
            at = lambda lst, h: lst[n * DN_HEADS + h]
            ws_qs = [_mm(jnp.concatenate([at(sol["x"], h)[:, DN_DV:], at(pro["q"], h) * at(pro["eg"], h)], axis=0),
                         state[h]) for h in heads]
            yield
            v_new = [at(sol["x"], h)[:, 0:DN_DV] - ws_qs[h][0:C] for h in heads]
            o = [ws_qs[h][C:] + _mm(at(sol["aqk"], h), v_new[h]) for h in heads]
            for h in heads:
                lane = DN_HEADS + h
                kd = at(pro["k"], h) * bcast(pro["dec"][c]["ek"], lane)
                dl = jnp.broadcast_to(pro["dec"][c]["dl"][:, lane:lane + 1], (DN_DK, DN_DV))
                state[h] = state[h] * dl + _mm_tn(kd, v_new[h])
            yield
            for h in heads:
                y = o[h] * lax.rsqrt(jnp.mean(o[h] * o[h], axis=-1, keepdims=True) + NORM_EPS) * nw
                gate = z_ref[base:base + C, h * DN_DV:(h + 1) * DN_DV].astype(F32)
                o_ref[base:base + C, h * DN_DV:(h + 1) * DN_DV] = (y * gate).astype(o_ref.dtype)
                if h % 4 == 3:
                    yield

    def interleave(gens):
        gens = list(gens)
        while gens:
            for g in list(gens):
                try:
                    next(g)
                except StopIteration:
                    gens.remove(g)

    n_chunks = tb // C
    units = [list(range(u, min(u + DN_UNIT, n_chunks))) for u in range(0, n_chunks, DN_UNIT)]
    pro = [dict() for _ in units]
    sol = [dict() for _ in units]
    for tick in range(len(units) + 2):
        active = []
        if 0 <= tick - 2 < len(units):
            active.append(recur(units[tick - 2], pro[tick - 2], sol[tick - 2]))
        if 0 <= tick - 1 < len(units):
            active.append(solve(units[tick - 1], pro[tick - 1], sol[tick - 1]))
        if tick < len(units):
            active.append(prologue(units[tick], pro[tick]))
        interleave(active)
    for h in heads:
        st_ref[h] = state[h]


def _deltanet_call(qkv, small, p2, a_log, dt_bias, dn_norm_w, B, S):
    tb = DN_TB
    nt = S // tb
    hp = jnp.zeros((8, LANES), F32)
    hp = hp.at[0, DN_HEADS:2 * DN_HEADS].set(a_log.astype(F32)).at[1, DN_HEADS:2 * DN_HEADS].set(dt_bias.astype(F32))
    nw = dn_norm_w.astype(F32).reshape(1, DN_DV)
    return pl.pallas_call(
        _deltanet_kernel,
        grid=(B, nt),
        in_specs=[pl.BlockSpec((tb, 3 * DN_W), lambda b, i: (b * nt + i, 0)),
                  pl.BlockSpec((tb, LANES), lambda b, i: (b * nt + i, 0)),
                  pl.BlockSpec((tb, DN_W), lambda b, i: (b * nt + i, 0)),
                  pl.BlockSpec((8, LANES), lambda b, i: (0, 0)),
                  pl.BlockSpec((1, DN_DV), lambda b, i: (0, 0))],
        out_specs=pl.BlockSpec((tb, DN_W), lambda b, i: (b * nt + i, 0)),
        out_shape=jax.ShapeDtypeStruct((B * S, DN_W), BF16),
        scratch_shapes=[pltpu.VMEM((DN_HEADS, DN_DK, DN_DV), F32)],
        compiler_params=pltpu.CompilerParams(dimension_semantics=("arbitrary", "arbitrary"),
                                             vmem_limit_bytes=VMEM_LIMIT),
        name="deltanet",
    )(qkv, small, p2, hp, nw)


def _alibi_slope(idx):
    n = N_DIL * DIL_HEADS
    return 2.0 ** (-8.0 * (idx + 1) / n)


def _attn_kernel(q0_ref, k0_ref, v0_ref, q1_ref, k1_ref, v1_ref, q2_ref, k2_ref, v2_ref, z_ref, o_ref,
                 num_ref, den_ref, mx_ref):
    Q = ATT_BLOCK
    hd = pl.program_id(1)
    span = DIL_GROUPS[0][0] // DIL_GROUPS[0][1]
    dist = (Q + lax.broadcasted_iota(jnp.int32, (Q, 2 * Q), 0) - lax.broadcasted_iota(jnp.int32, (Q, 2 * Q), 1))
    valid = (dist >= 0) & (dist <= span)
    distf = dist.astype(F32)

    def slope_of(g):
        s = jnp.float32(_alibi_slope(g * DIL_HEADS))
        for h in range(1, DIL_HEADS):
            s = jnp.where(hd == h, jnp.float32(_alibi_slope(g * DIL_HEADS + h)), s)
        return s

    def group_batches(g, q_ref, k_ref, v_ref, emit):
        d = DIL_GROUPS[g][1]
        bias2 = jnp.where(valid, -(slope_of(g) * (float(d) * LOG2E)) * distf, -jnp.inf)
        bias1 = bias2[:, Q:]
        nb = q_ref.shape[-2] // Q

        def item(r, j):
            qr, kr, vr = (q_ref, k_ref, v_ref) if d == 1 else (q_ref.at[r], k_ref.at[r], v_ref.at[r])
            if j == 0:
                return qr[0:Q, :], kr[0:Q, :], vr[0:Q, :], bias1
            lo, cur = (j - 1) * Q, j * Q
            return qr[cur:cur + Q, :], kr[lo:lo + 2 * Q, :], vr[lo:lo + 2 * Q, :], bias2

        def batch(ids):
            items = [item(r, j) for r, j in ids]
            s = [_mm_nt(q, k) + bias for q, k, v, bias in items]
            mx = [jnp.max(si, axis=-1, keepdims=True) for si in s]
            yield
            p = [jnp.exp2(si - mi) for si, mi in zip(s, mx)]
            den = [jnp.sum(pi, axis=-1, keepdims=True) for pi in p]
            num = [_mm(pi, it[2]) for pi, it in zip(p, items)]
            yield
            for (r, j), n, dn, m in zip(ids, num, den, mx):
                emit(r, j, (n, jnp.broadcast_to(dn, (Q, LANES)), jnp.broadcast_to(m, (Q, LANES))))

        ids = [(r, j) for r in range(d) for j in range(nb)]
        return [batch(ids[i:i + ATT_BATCH]) for i in range(0, len(ids), ATT_BATCH)]

    def rows(g, r, j):
        d = DIL_GROUPS[g][1]
        if d == 1:
            return pl.ds(j * Q, Q)
        return pl.ds(j * (Q * d) + r, Q, stride=d)

    def store_to(g, slot):
        def emit(r, j, res):
            idx = rows(g, r, j)
            num_ref[slot, idx, :] = res[0]
            den_ref[slot, idx, :] = res[1]
            mx_ref[slot, idx, :] = res[2]
        return emit

    def emit_last(r, j, res):
        idx = rows(0, r, j)
        parts = [(num_ref[s, idx, :], den_ref[s, idx, :], mx_ref[s, idx, :]) for s in range(2)] + [res]
        m = jnp.maximum(jnp.maximum(parts[0][2], parts[1][2]), parts[2][2])
        w = [jnp.exp2(p[2] - m) for p in parts]
        n = parts[0][0] * w[0] + parts[1][0] * w[1] + parts[2][0] * w[2]
        dn = parts[0][1] * w[0] + parts[1][1] * w[1] + parts[2][1] * w[2]
        o_ref[idx, :] = ((n / dn) * z_ref[idx, :].astype(F32)).astype(o_ref.dtype)

    pending = (group_batches(2, q2_ref, k2_ref, v2_ref, store_to(2, 0))
               + group_batches(1, q1_ref, k1_ref, v1_ref, store_to(1, 1))
               + group_batches(0, q0_ref, k0_ref, v0_ref, emit_last))
    active = []
    while pending or active:
        if pending:
            active.append(pending.pop(0))
        for gen in list(active):
            try:
                next(gen)
            except StopIteration:
                active.remove(gen)


def _attn_call(p2, off0, zb_off, g1, g2, B, S):
    H = DIL_HEADS
    p2v = p2.reshape(B, S, p2.shape[1])
    g1v = g1.reshape(B, 4, S // 4, GROUP_W)
    g2v = g2.reshape(B, 16, S // 16, GROUP_W)
    b0 = off0 // LANES
    zb0 = zb_off // LANES

    def nat(cb):
        return pl.BlockSpec((None, S, LANES), lambda b, h: (b, 0, cb + h))

    def perm(d, cb):
        return pl.BlockSpec((None, d, S // d, LANES), lambda b, h: (b, 0, 0, cb + h))

    return pl.pallas_call(
        _attn_kernel,
        grid=(B, H),
        in_specs=[nat(b0), nat(b0 + H), nat(b0 + 2 * H),
                  perm(4, 0), perm(4, H), perm(4, 2 * H),
                  perm(16, 0), perm(16, H), perm(16, 2 * H),
                  nat(zb0)],
        out_specs=pl.BlockSpec((None, S, LANES), lambda b, h: (b, 0, h)),
        out_shape=jax.ShapeDtypeStruct((B, S, DIL_W), BF16),
        scratch_shapes=[pltpu.VMEM((2, S, LANES), F32)] * 3,
        compiler_params=pltpu.CompilerParams(dimension_semantics=("arbitrary", "arbitrary"),
                                             vmem_limit_bytes=VMEM_LIMIT),
        name="dilated_attention",
    )(p2v, p2v, p2v, g1v, g1v, g1v, g2v, g2v, g2v, p2v)


def _out_kernel(oa_ref, ob_ref, ga_ref, gb_ref, x_ref, wa_ref, wb_ref, wo_ref, fw_ref, o_ref, *, final):
    ya = jnp.dot(oa_ref[...], wa_ref[...], preferred_element_type=F32)
    yb = jnp.dot(ob_ref[...], wb_ref[...], preferred_element_type=F32)
    merged = ga_ref[...].astype(F32) * ya + gb_ref[...].astype(F32) * yb
    xn = x_ref[...] + jnp.dot(merged.astype(BF16), wo_ref[...], preferred_element_type=F32)
    if final:
        xn = xn * lax.rsqrt(jnp.mean(xn * xn, axis=-1, keepdims=True) + NORM_EPS) * fw_ref[...]
    o_ref[...] = xn


def _out_call(oa, ob, p2, ga_off, gb_off, x2d, wa, wb, wo, fw, final):
    T, D = x2d.shape
    tm = OUT_TM
    ga_b = ga_off // D
    gb_b = gb_off // D
    const = lambda i: (0, 0)
    return pl.pallas_call(
        functools.partial(_out_kernel, final=final),
        grid=(T // tm,),
        in_specs=[pl.BlockSpec((tm, DN_W), lambda i: (i, 0)),
                  pl.BlockSpec((tm, DIL_W), lambda i: (i, 0)),
                  pl.BlockSpec((tm, D), lambda i: (i, ga_b)),
                  pl.BlockSpec((tm, D), lambda i: (i, gb_b)),
                  pl.BlockSpec((tm, D), lambda i: (i, 0)),
                  pl.BlockSpec((DN_W, D), const),
                  pl.BlockSpec((DIL_W, D), const),
                  pl.BlockSpec((D, D), const),
                  pl.BlockSpec((1, D), const)],
        out_specs=pl.BlockSpec((tm, D), lambda i: (i, 0)),
        out_shape=jax.ShapeDtypeStruct((T, D), F32),
        compiler_params=pltpu.CompilerParams(dimension_semantics=("arbitrary",),
                                             vmem_limit_bytes=VMEM_LIMIT),
        name="out_merge",
    )(oa, ob, p2, p2, x2d, wa, wb, wo, fw)


def _split_cols(w):
    out, start = [], 0
    for s in PROJ_SIZES:
        out.append(w[:, start:start + s])
        start += s
    return out


def _layer(x, norm_w, w_in, conv_w, a_log, dt_bias, dn_norm_w, w_o_dn, w_o_dil, w_out, final_w, final):
    B, S, D = x.shape
    T = B * S
    (wq_a, wk_a, wv_a, wz_a, wb_a, wa_a, wq_b, wk_b, wv_b, wz_b, wg_a, wg_b) = _split_cols(w_in)
    def group_w(g):
        sl = slice(g * DIL_W, (g + 1) * DIL_W)
        wq = wq_b[:, sl] * (DIL_DH ** -0.5 * LOG2E)
        return [wq, wk_b[:, sl], wv_b[:, sl]]

    w_all = jnp.concatenate(group_w(1) + group_w(2) + [wq_a, wk_a, wv_a] + [wz_a, wg_a, wg_b, wz_b] + group_w(0)
                            + [wb_a, wa_a, jnp.zeros((D, LANES - 2 * DN_HEADS), w_in.dtype)], axis=1).astype(BF16)
    g1_col, g2_col, w1_col = 0, GROUP_W, 2 * GROUP_W
    w2_col = w1_col + 3 * DN_W
    small_col = w2_col + 5 * D
    ga_off = DN_W
    gb_off = ga_off + D
    zb_off = gb_off + D
    off0 = zb_off + DIL_W
    tn = D
    tiles2 = (((0, tn, "silu"),), ((0, tn, "sigmoid"),), ((0, tn, "sigmoid"),),
              ((0, DIL_W, "silu"), (DIL_W, tn, "none")), ((0, tn, "none"),))
    plain = (((0, GROUP_W, "none"),),)

    h, h4, h16, small = _norm_call(x, norm_w.astype(F32), w_all, small_col)
    qkv = _proj_conv_call(h.reshape(T, D), w_all, w1_col, 3 * DN_W, conv_w, S, "proj_conv_deltanet")
    p2 = _proj_call(h.reshape(T, D), w_all, w2_col, tn, tiles2, BF16, "proj_gates_group0")
    g1 = _proj_call(h4.reshape(T, D), w_all, g1_col, GROUP_W, plain, BF16, "proj_group1")
    g2 = _proj_call(h16.reshape(T, D), w_all, g2_col, GROUP_W, plain, BF16, "proj_group2")

    o_a = _deltanet_call(qkv, small.reshape(T, LANES), p2, a_log, dt_bias, dn_norm_w, B, S)
    o_b = _attn_call(p2, off0, zb_off, g1, g2, B, S)
    out = _out_call(o_a, o_b.reshape(T, DIL_W), p2, ga_off, gb_off, x.reshape(T, D),
                    w_o_dn.astype(BF16), w_o_dil.astype(BF16), w_out.astype(BF16),
                    final_w.astype(F32).reshape(1, D), final)
    return out.reshape(B, S, D)


def kernel(x, norm_w, w_in, conv_w, a_log, dt_bias, dn_norm_w, w_o_dn, w_o_dil, w_out, final_norm_w):
    depth = norm_w.shape[0]
    for l in range(depth):
        x = _layer(x, norm_w[l], w_in[l], conv_w[l], a_log[l], dt_bias[l], dn_norm_w[l],
                   w_o_dn[l], w_o_dil[l], w_out[l], final_norm_w, final=(l == depth - 1))
    return x
```

```python
import functools
import math

import jax
import jax.numpy as jnp
import numpy as np
from jax import lax
from jax.experimental import pallas as pl
from jax.experimental.pallas import tpu as pltpu

F32 = jnp.float32
BF16 = jnp.bfloat16

D_MODEL = 1024
DN_HEADS = 8
DN_DK = 128
DN_DV = 128
DN_CONV = 4
DN_CHUNK = 64
DIL_GROUPS = ((128, 1), (512, 4), (2048, 16))
DIL_HEADS = 4
DIL_DH = 128
ATT_BLOCK = 128
NORM_EPS = 1e-6
LOG2E = math.log2(math.e)
N_DIL = len(DIL_GROUPS)
DN_W = DN_HEADS * DN_DK
DIL_W = DIL_HEADS * DIL_DH
GROUP_W = 3 * DIL_W
PROJ_SIZES = (DN_W, DN_W, DN_W, DN_W, DN_HEADS, DN_HEADS,
              N_DIL * DIL_W, N_DIL * DIL_W, N_DIL * DIL_W, DIL_W, D_MODEL, D_MODEL)

LANES = 128
VMEM_LIMIT = 56 * 1024 * 1024

NORM_TM = 512
PROJ_TM = 2048
PROJ_TN = 1024
PROJ_SUB = 512
CONV_SUB = 256
DN_TB = 512
DN_UNIT = 2
ATT_BATCH = 8
OUT_TM = 1024


def _sigmoid(x):
    return 1.0 / (1.0 + jnp.exp2(x * (-LOG2E)))


def _silu(x):
    return x * _sigmoid(x)


def _softplus(x):
    return jnp.maximum(x, 0.0) + jnp.log1p(jnp.exp(-jnp.abs(x)))


def _mm(a, b):
    return lax.dot_general(a.astype(BF16), b.astype(BF16), (((1,), (0,)), ((), ())),
                           preferred_element_type=F32)


def _mm_nt(a, b):
    return lax.dot_general(a.astype(BF16), b.astype(BF16), (((1,), (1,)), ((), ())),
                           preferred_element_type=F32)


def _mm_tn(a, b):
    return lax.dot_general(a.astype(BF16), b.astype(BF16), (((0,), (0,)), ((), ())),
                           preferred_element_type=F32)


def _norm_kernel(x_ref, w_ref, ws_ref, h_ref, h4_ref, h16_ref, sm_ref, slab_ref):
    x = x_ref[...]
    y = x * lax.rsqrt(jnp.mean(x * x, axis=-1, keepdims=True) + NORM_EPS) * w_ref[...]
    hb = y.astype(BF16)
    h_ref[...] = hb
    sm_ref[...] = jnp.dot(hb, ws_ref[...], preferred_element_type=F32)
    n_slabs = D_MODEL // LANES
    for c in range(n_slabs):
        slab_ref[c] = y[:, c * LANES:(c + 1) * LANES]
    tm = x.shape[0]
    for d, out in ((4, h4_ref), (16, h16_ref)):
        n = tm // d
        for r in range(d):
            for c in range(n_slabs):
                out[r, :, c * LANES:(c + 1) * LANES] = slab_ref[c, pl.ds(r, n, stride=d), :].astype(BF16)


def _norm_call(x, w, w_all, small_col):
    B, S, D = x.shape
    tm = NORM_TM
    nt = S // tm
    small_blk = small_col // LANES
    return pl.pallas_call(
        _norm_kernel,
        grid=(B, nt),
        in_specs=[pl.BlockSpec((None, tm, D), lambda b, i: (b, i, 0)),
                  pl.BlockSpec((1, D), lambda b, i: (0, 0)),
                  pl.BlockSpec((D, LANES), lambda b, i: (0, small_blk))],
        out_specs=[pl.BlockSpec((None, tm, D), lambda b, i: (b, i, 0)),
                   pl.BlockSpec((None, 4, tm // 4, D), lambda b, i: (b, 0, i, 0)),
                   pl.BlockSpec((None, 16, tm // 16, D), lambda b, i: (b, 0, i, 0)),
                   pl.BlockSpec((None, tm, LANES), lambda b, i: (b, i, 0))],
        out_shape=[jax.ShapeDtypeStruct((B, S, D), BF16),
                   jax.ShapeDtypeStruct((B, 4, S // 4, D), BF16),
                   jax.ShapeDtypeStruct((B, 16, S // 16, D), BF16),
                   jax.ShapeDtypeStruct((B, S, LANES), F32)],
        scratch_shapes=[pltpu.VMEM((D // LANES, tm, LANES), F32)],
        compiler_params=pltpu.CompilerParams(dimension_semantics=("arbitrary", "arbitrary"),
                                             vmem_limit_bytes=VMEM_LIMIT),
        name="norm_deinterleave",
    )(x, w.reshape(1, D), w_all)


_ACTS = {"none": lambda v: v, "silu": _silu, "sigmoid": _sigmoid}


def _proj_kernel(h_ref, w_ref, o_ref, *, tiles):
    tm = h_ref.shape[0]
    sub = PROJ_SUB

    def run(parts):
        for m in range(tm // sub):
            rows = slice(m * sub, (m + 1) * sub)
            acc = jnp.dot(h_ref[rows, :], w_ref[...], preferred_element_type=F32)
            for lo, hi, act in parts:
                o_ref[rows, lo:hi] = _ACTS[act](acc[:, lo:hi]).astype(o_ref.dtype)

    kinds = sorted(set(tiles), key=tiles.index)
    if len(kinds) == 1:
        run(kinds[0])
        return
    j = pl.program_id(1)
    for parts in kinds:
        cond = functools.reduce(lambda a, b: a | b, [j == t for t, p in enumerate(tiles) if p == parts])
        pl.when(cond)(functools.partial(run, parts))


def _proj_call(h2d, w_all, col0, tn, tiles, out_dtype, name):
    T, D = h2d.shape
    N = len(tiles) * tn
    tm = PROJ_TM
    assert col0 % tn == 0 and T % tm == 0
    blk0 = col0 // tn
    return pl.pallas_call(
        functools.partial(_proj_kernel, tiles=tiles),
        grid=(T // tm, N // tn),
        in_specs=[pl.BlockSpec((tm, D), lambda i, j: (i, 0)),
                  pl.BlockSpec((D, tn), lambda i, j: (0, blk0 + j))],
        out_specs=pl.BlockSpec((tm, tn), lambda i, j: (i, j)),
        out_shape=jax.ShapeDtypeStruct((T, N), out_dtype),
        compiler_params=pltpu.CompilerParams(dimension_semantics=("arbitrary", "arbitrary"),
                                             vmem_limit_bytes=VMEM_LIMIT),
        name=name,
    )(h2d, w_all)


def _proj_conv_kernel(h_ref, w_ref, cw_ref, o_ref, u_ref, y_ref, *, tiles_per_seq):
    i = pl.program_id(1)
    tm = h_ref.shape[0]
    sub = CONV_SUB
    half = sub // 2
    n_slab = u_ref.shape[0]

    @pl.when(i % tiles_per_seq == 0)
    def _():
        u_ref[:, 0:8, :] = jnp.zeros((n_slab, 8, LANES), F32)

    @pl.when(i % tiles_per_seq != 0)
    def _():
        u_ref[:, 0:8, :] = u_ref[:, tm:tm + 8, :]

    for m in range(tm // sub):
        base = 8 + m * sub
        acc = jnp.dot(h_ref[m * sub:(m + 1) * sub, :], w_ref[...], preferred_element_type=F32)
        for c in range(n_slab):
            u_ref[c, base:base + sub, :] = acc[:, c * LANES:(c + 1) * LANES]
        for c in range(n_slab):
            taps = [cw_ref[k:k + 1, c * LANES:(c + 1) * LANES] for k in range(DN_CONV)]
            for phase in range(2):
                first = base + phase - (DN_CONV - 1)
                y = taps[0] * u_ref[c, pl.ds(first, half, stride=2), :]
                for k in range(1, DN_CONV):
                    y = y + taps[k] * u_ref[c, pl.ds(first + k, half, stride=2), :]
                y_ref[c, pl.ds(phase, half, stride=2), :] = y + y * jnp.tanh(y)
            o_ref[m * sub:(m + 1) * sub, c * LANES:(c + 1) * LANES] = y_ref[c].astype(o_ref.dtype)


def _proj_conv_call(h2d, w_all, col0, N, conv_w, S, name):
    T, D = h2d.shape
    tm, tn = PROJ_TM, PROJ_TN
    assert N % tn == 0 and col0 % tn == 0 and T % tm == 0 and S % tm == 0
    blk0 = col0 // tn
    cw = jnp.zeros((8, N), F32).at[0:DN_CONV].set(0.5 * conv_w.astype(F32))
    return pl.pallas_call(
        functools.partial(_proj_conv_kernel, tiles_per_seq=S // tm),
        grid=(N // tn, T // tm),
        in_specs=[pl.BlockSpec((tm, D), lambda j, i: (i, 0)),
                  pl.BlockSpec((D, tn), lambda j, i: (0, blk0 + j)),
                  pl.BlockSpec((8, tn), lambda j, i: (0, j))],
        out_specs=pl.BlockSpec((tm, tn), lambda j, i: (i, j)),
        out_shape=jax.ShapeDtypeStruct((T, N), BF16),
        scratch_shapes=[pltpu.VMEM((tn // LANES, tm + 8, LANES), F32),
                        pltpu.VMEM((tn // LANES, CONV_SUB, LANES), F32)],
        compiler_params=pltpu.CompilerParams(dimension_semantics=("arbitrary", "arbitrary"),
                                             vmem_limit_bytes=VMEM_LIMIT),
        name=name,
    )(h2d, w_all, cw)


def _deltanet_kernel(qkv_ref, sm_ref, z_ref, hp_ref, nw_ref, o_ref, st_ref):
    C = DN_CHUNK
    tb = o_ref.shape[0]

    @pl.when(pl.program_id(1) == 0)
    def _():
        st_ref[...] = jnp.zeros_like(st_ref)

    row = lax.broadcasted_iota(jnp.int32, (C, C), 0)
    col = lax.broadcasted_iota(jnp.int32, (C, C), 1)
    causal = row >= col
    strict = row > col
    eye =(row == col).astype(F32)
    hp = hp_ref[...]
    neg_a = -jnp.exp(hp[0:1, :])
    dt_b = hp[1:2, :]
    nw = nw_ref[...]

    heads = range(DN_HEADS)

    def bcast(a, lane):
        return jnp.broadcast_to(a[:, lane:lane + 1], (C, LANES))

    def load(base, c0):
        return qkv_ref[base:base + C, c0:c0 + LANES].astype(F32)

    row_l = lax.broadcasted_iota(jnp.int32, (C, LANES), 0)

    def cumsum_rows(v):
        shift = 1
        while shift < C:
            v = v + jnp.where(row_l >= shift, pltpu.roll(v, shift, axis=0), 0.0)
            shift *= 2
        return v

    def prologue(unit, out):
        dec = {}
        for c in unit:
            sm = sm_ref[c * C:(c + 1) * C, :]
            g_all = neg_a * _softplus(sm + dt_b)
            gc_all = cumsum_rows(g_all)
            gc_t = jnp.concatenate([gc_all, jnp.zeros_like(gc_all)], axis=0).T
            glast_all = gc_all[C - 1:C, :]
            dec[c] = dict(beta=_sigmoid(sm), gc=gc_all, gc_t=gc_t, eg=jnp.exp(gc_all),
                          ek=jnp.exp(glast_all - gc_all), dl=jnp.exp(glast_all))
            yield
        q, k, kb, eg, gamma, rhs = [], [], [], [], [], []
        for c in unit:
            for h in heads:
                lane = DN_HEADS + h
                base = c * C
                qh = load(base, h * DN_DK)
                kh = load(base, DN_W + h * DN_DK)
                vh = load(base, 2 * DN_W + h * DN_DV)
                qh = qh * (lax.rsqrt(jnp.sum(qh * qh, axis=-1, keepdims=True) + NORM_EPS) * (DN_DK ** -0.5))
                kh = kh * lax.rsqrt(jnp.sum(kh * kh, axis=-1, keepdims=True) + NORM_EPS)
                beta = bcast(dec[c]["beta"], h)
                egh = bcast(dec[c]["eg"], lane)
                kbh = kh * beta
                q.append(qh)
                k.append(kh)
                kb.append(kbh)
                eg.append(egh)
                rhs.append(jnp.concatenate([vh * beta, kbh * egh], axis=1))
                diff = bcast(dec[c]["gc"], lane)[:, 0:C] - dec[c]["gc_t"][lane:lane + 1, 0:C]
                gamma.append(jnp.exp(jnp.where(causal, diff, -jnp.inf)))
                if h % 2 == 1:
                    yield
        out.update(dec=dec, q=q, k=k, kb=kb, eg=eg, gamma=gamma, rhs=rhs)

    def solve(unit, pro, out):
        idx = range(len(unit) * DN_HEADS)
        q, k, kb, gamma, rhs = pro["q"], pro["k"], pro["kb"], pro["gamma"], pro["rhs"]
        a = [_mm_nt(jnp.concatenate([kb[i], q[i]], axis=0), k[i]) for i in idx]
        yield
        p = [jnp.where(strict, -(a[i][0:C] * gamma[i]), 0.0) for i in idx]
        aqk = [a[i][C:] * gamma[i] for i in idx]
        t = [eye + p[i] for i in idx]
        p = [_mm(p[i], p[i]) for i in idx]
        yield
        for _ in range(4):
            r = [_mm(jnp.concatenate([p[i], t[i]], axis=0), p[i]) for i in idx]
            t = [t[i] + r[i][C:] for i in idx]
            p = [r[i][0:C] for i in idx]
            yield
        t = [t[i] + _mm(t[i], p[i]) for i in idx]
        yield
        out.update(x=[_mm(t[i], rhs[i]) for i in idx], aqk=aqk)

    state = [st_ref[h] for h in heads]

    def recur(unit, pro, sol):
        for n, c in enumerate(unit):
            base = c * C
            at = lambda lst, h: lst[n * DN_HEADS + h]
            ws_qs = [_mm(jnp.concatenate([at(sol["x"], h)[:, DN_DV:], at(pro["q"], h) * at(pro["eg"], h)], axis=0),
                         state[h]) for h in heads]
            yield
            v_new = [at(sol["x"], h)[:, 0:DN_DV] - ws_qs[h][0:C] for h in heads]
            o = [ws_qs[h][C:] + _mm(at(sol["aqk"], h), v_new[h]) for h in heads]
            for h in heads:
                lane = DN_HEADS + h
                kd = at(pro["k"], h) * bcast(pro["dec"][c]["ek"], lane)
                dl = jnp.broadcast_to(pro["dec"][c]["dl"][:, lane:lane + 1], (DN_DK, DN_DV))
                state[h] = state[h] * dl + _mm_tn(kd, v_new[h])
            yield
            for h in heads:
                y = o[h] * lax.rsqrt(jnp.mean(o[h] * o[h], axis=-1, keepdims=True) + NORM_EPS) * nw
                gate = z_ref[base:base + C, h * DN_DV:(h + 1) * DN_DV].astype(F32)
                o_ref[base:base + C, h * DN_DV:(h + 1) * DN_DV] = (y * gate).astype(o_ref.dtype)
                if h % 4 == 3:
                    yield

    def interleave(gens):
        gens = list(gens)
        while gens:
            for g in list(gens):
                try:
                    next(g)
                except StopIteration:
                    gens.remove(g)

    n_chunks = tb // C
    units = [list(range(u, min(u + DN_UNIT, n_chunks))) for u in range(0, n_chunks, DN_UNIT)]
    pro = [dict() for _ in units]
    sol = [dict() for _ in units]
    for tick in range(len(units) + 2):
        active = []
        if 0 <= tick - 2 < len(units):
            active.append(recur(units[tick - 2], pro[tick - 2], sol[tick - 2]))
        if 0 <= tick - 1 < len(units):
            active.append(solve(units[tick - 1], pro[tick - 1], sol[tick - 1]))
        if tick < len(units):
            active.append(prologue(units[tick], pro[tick]))
        interleave(active)
    for h in heads:
        st_ref[h] = state[h]


def _deltanet_call(qkv, small, p2, a_log, dt_bias, dn_norm_w, B, S):
    tb = DN_TB
    nt = S // tb
    hp = jnp.zeros((8, LANES), F32)
    hp = hp.at[0, DN_HEADS:2 * DN_HEADS].set(a_log.astype(F32)).at[1, DN_HEADS:2 * DN_HEADS].set(dt_bias.astype(F32))
    nw = dn_norm_w.astype(F32).reshape(1, DN_DV)
    return pl.pallas_call(
        _deltanet_kernel,
        grid=(B, nt),
        in_specs=[pl.BlockSpec((tb, 3 * DN_W), lambda b, i: (b * nt + i, 0)),
                  pl.BlockSpec((tb, LANES), lambda b, i: (b * nt + i, 0)),
                  pl.BlockSpec((tb, DN_W), lambda b, i: (b * nt + i, 0)),
                  pl.BlockSpec((8, LANES), lambda b, i: (0, 0)),
                  pl.BlockSpec((1, DN_DV), lambda b, i: (0, 0))],
        out_specs=pl.BlockSpec((tb, DN_W), lambda b, i: (b * nt + i, 0)),
        out_shape=jax.ShapeDtypeStruct((B * S, DN_W), BF16),
        scratch_shapes=[pltpu.VMEM((DN_HEADS, DN_DK, DN_DV), F32)],
        compiler_params=pltpu.CompilerParams(dimension_semantics=("arbitrary", "arbitrary"),
                                             vmem_limit_bytes=VMEM_LIMIT),
        name="deltanet",
    )(qkv, small, p2, hp, nw)


def _alibi_slope(idx):
    n = N_DIL * DIL_HEADS
    return 2.0 ** (-8.0 * (idx + 1) / n)


def _attn_kernel(q0_ref, k0_ref, v0_ref, q1_ref, k1_ref, v1_ref, q2_ref, k2_ref, v2_ref, z_ref, o_ref,
                 num_ref, den_ref, mx_ref):
    Q = ATT_BLOCK
    hd = pl.program_id(1)
    span = DIL_GROUPS[0][0] // DIL_GROUPS[0][1]
    dist = (Q + lax.broadcasted_iota(jnp.int32, (Q, 2 * Q), 0) - lax.broadcasted_iota(jnp.int32, (Q, 2 * Q), 1))
    valid = (dist >= 0) & (dist <= span)
    distf = dist.astype(F32)

    def slope_of(g):
        s = jnp.float32(_alibi_slope(g * DIL_HEADS))
        for h in range(1, DIL_HEADS):
            s = jnp.where(hd == h, jnp.float32(_alibi_slope(g * DIL_HEADS + h)), s)
        return s

    def group_batches(g, q_ref, k_ref, v_ref, emit):
        d = DIL_GROUPS[g][1]
        bias2 = jnp.where(valid, -(slope_of(g) * (float(d) * LOG2E)) * distf, -jnp.inf)
        bias1 = bias2[:, Q:]
        nb = q_ref.shape[-2] // Q

        def item(r, j):
            qr, kr, vr = (q_ref, k_ref, v_ref) if d == 1 else (q_ref.at[r], k_ref.at[r], v_ref.at[r])
            if j == 0:
                return qr[0:Q, :], kr[0:Q, :], vr[0:Q, :], bias1
            lo, cur = (j - 1) * Q, j * Q
            return qr[cur:cur + Q, :], kr[lo:lo + 2 * Q, :], vr[lo:lo + 2 * Q, :], bias2

        def batch(ids):
            items = [item(r, j) for r, j in ids]
            s = [_mm_nt(q, k) + bias for q, k, v, bias in items]
            mx = [jnp.max(si, axis=-1, keepdims=True) for si in s]
            yield
            p = [jnp.exp2(si - mi) for si, mi in zip(s, mx)]
            den = [jnp.sum(pi, axis=-1, keepdims=True) for pi in p]
            num = [_mm(pi, it[2]) for pi, it in zip(p, items)]
            yield
            for (r, j), n, dn, m in zip(ids, num, den, mx):
                emit(r, j, (n, jnp.broadcast_to(dn, (Q, LANES)), jnp.broadcast_to(m, (Q, LANES))))

        ids = [(r, j) for r in range(d) for j in range(nb)]
        return [batch(ids[i:i + ATT_BATCH]) for i in range(0, len(ids), ATT_BATCH)]

    def rows(g, r, j):
        d = DIL_GROUPS[g][1]
        if d == 1:
            return pl.ds(j * Q, Q)
        return pl.ds(j * (Q * d) + r, Q, stride=d)

    def store_to(g, slot):
        def emit(r, j, res):
            idx = rows(g, r, j)
            num_ref[slot, idx, :] = res[0]
            den_ref[slot, idx, :] = res[1]
            mx_ref[slot, idx, :] = res[2]
        return emit

    def emit_last(r, j, res):
        idx = rows(0, r, j)
        parts = [(num_ref[s, idx, :], den_ref[s, idx, :], mx_ref[s, idx, :]) for s in range(2)] + [res]
        m = jnp.maximum(jnp.maximum(parts[0][2], parts[1][2]), parts[2][2])
        w = [jnp.exp2(p[2] - m) for p in parts]
        n = parts[0][0] * w[0] + parts[1][0] * w[1] + parts[2][0] * w[2]
        dn = parts[0][1] * w[0] + parts[1][1] * w[1] + parts[2][1] * w[2]
        o_ref[idx, :] = ((n / dn) * z_ref[idx, :].astype(F32)).astype(o_ref.dtype)

    pending = (group_batches(2, q2_ref, k2_ref, v2_ref, store_to(2, 0))
               + group_batches(1, q1_ref, k1_ref, v1_ref, store_to(1, 1))
               + group_batches(0, q0_ref, k0_ref, v0_ref, emit_last))
    active = []
    while pending or active:
        if pending:
            active.append(pending.pop(0))
        for gen in list(active):
            try:
                next(gen)
            except StopIteration:
                active.remove(gen)


def _attn_call(p2, off0, zb_off, g1, g2, B, S):
    H = DIL_HEADS
    p2v = p2.reshape(B, S, p2.shape[1])
    g1v = g1.reshape(B, 4, S // 4, GROUP_W)
    g2v = g2.reshape(B, 16, S // 16, GROUP_W)
    b0 = off0 // LANES
    zb0 = zb_off // LANES

    def nat(cb):
        return pl.BlockSpec((None, S, LANES), lambda b, h: (b, 0, cb + h))

    def perm(d, cb):
        return pl.BlockSpec((None, d, S // d, LANES), lambda b, h: (b, 0, 0, cb + h))

    return pl.pallas_call(
        _attn_kernel,
        grid=(B, H),
        in_specs=[nat(b0), nat(b0 + H), nat(b0 + 2 * H),
                  perm(4, 0), perm(4, H), perm(4, 2 * H),
                  perm(16, 0), perm(16, H), perm(16, 2 * H),
                  nat(zb0)],
        out_specs=pl.BlockSpec((None, S, LANES), lambda b, h: (b, 0, h)),
        out_shape=jax.ShapeDtypeStruct((B, S, DIL_W), BF16),
        scratch_shapes=[pltpu.VMEM((2, S, LANES), F32)] * 3,
        compiler_params=pltpu.CompilerParams(dimension_semantics=("arbitrary", "arbitrary"),
                                             vmem_limit_bytes=VMEM_LIMIT),
        name="dilated_attention",
    )(p2v, p2v, p2v, g1v, g1v, g1v, g2v, g2v, g2v, p2v)


def _out_kernel(oa_ref, ob_ref, ga_ref, gb_ref, x_ref, wa_ref, wb_ref, wo_ref, fw_ref, o_ref, *, final):
    ya = jnp.dot(oa_ref[...], wa_ref[...], preferred_element_type=F32)
    yb = jnp.dot(ob_ref[...], wb_ref[...], preferred_element_type=F32)
    merged = ga_ref[...].astype(F32) * ya + gb_ref[...].astype(F32) * yb
    xn = x_ref[...] + jnp.dot(merged.astype(BF16), wo_ref[...], preferred_element_type=F32)
    if final:
        xn = xn * lax.rsqrt(jnp.mean(xn * xn, axis=-1, keepdims=True) + NORM_EPS) * fw_ref[...]
    o_ref[...] = xn


def _out_call(oa, ob, p2, ga_off, gb_off, x2d, wa, wb, wo, fw, final):
    T, D = x2d.shape
    tm = OUT_TM
    ga_b = ga_off // D
    gb_b = gb_off // D
    const = lambda i: (0, 0)
    return pl.pallas_call(
        functools.partial(_out_kernel, final=final),
        grid=(T // tm,),
        in_specs=[pl.BlockSpec((tm, DN_W), lambda i: (i, 0)),
                  pl.BlockSpec((tm, DIL_W), lambda i: (i, 0)),
                  pl.BlockSpec((tm, D), lambda i: (i, ga_b)),
                  pl.BlockSpec((tm, D), lambda i: (i, gb_b)),
                  pl.BlockSpec((tm, D), lambda i: (i, 0)),
                  pl.BlockSpec((DN_W, D), const),
                  pl.BlockSpec((DIL_W, D), const),
                  pl.BlockSpec((D, D), const),
                  pl.BlockSpec((1, D), const)],
        out_specs=pl.BlockSpec((tm, D), lambda i: (i, 0)),
        out_shape=jax.ShapeDtypeStruct((T, D), F32),
        compiler_params=pltpu.CompilerParams(dimension_semantics=("arbitrary",),
                                             vmem_limit_bytes=VMEM_LIMIT),
        name="out_merge",
    )(oa, ob, p2, p2, x2d, wa, wb, wo, fw)


WPREP_TN = 512
WPREP_SRC_BLOCKS = WPREP_TN // LANES + 1


def _wprep_kernel(tbl_ref, *refs):
    src_refs, o_ref = refs[:WPREP_SRC_BLOCKS], refs[WPREP_SRC_BLOCKS]
    t = pl.program_id(0)
    shift = tbl_ref[1, t]
    kind = tbl_ref[2, t]
    x = jnp.concatenate([r[...] for r in src_refs], axis=1)
    lane = lax.broadcasted_iota(jnp.int32, (1, WPREP_TN), 1)
    factor = jnp.where(kind == 1, jnp.float32(DIL_DH ** -0.5 * LOG2E), jnp.float32(1.0))
    keep = jnp.where((kind != 2) | (lane < 2 * DN_HEADS), factor, 0.0)
    for off in WPREP_SHIFTS:
        @pl.when(shift == off)
        def _(off=off):
            o_ref[...] = (x[:, off:off + WPREP_TN] * keep).astype(o_ref.dtype)


def _wprep_plan():
    starts, acc = {}, 0
    for name, size in zip(("q_a", "k_a", "v_a", "z_a", "b_a", "a_a", "q_b", "k_b", "v_b", "z_b", "g_a", "g_b"), PROJ_SIZES):
        starts[name] = acc
        acc += size

    def group(g):
        return [(starts[n] + g * DIL_W, 1 if n == "q_b" else 0) for n in ("q_b", "k_b", "v_b")]

    def span(name, width):
        return [(starts[name] + o, 0) for o in range(0, width, WPREP_TN)]

    plan = (group(1) + group(2) + span("q_a", DN_W) + span("k_a", DN_W) + span("v_a", DN_W) + span("z_a", DN_W)
            + span("g_a", D_MODEL) + span("g_b", D_MODEL) + span("z_b", DIL_W) + group(0) + [(starts["b_a"], 2)])
    return plan, acc


_WPREP_PLAN, _PROJ_W = _wprep_plan()
WPREP_SHIFTS = tuple(sorted({src % LANES for src, _ in _WPREP_PLAN}))


def _wprep_call(w_in, layer):
    D = w_in.shape[1]
    assert w_in.shape[2] == _PROJ_W
    tbl = jnp.asarray(np.array([[s // LANES for s, _ in _WPREP_PLAN], [s % LANES for s, _ in _WPREP_PLAN],
                                [k for _, k in _WPREP_PLAN]], dtype=np.int32))
    n_tiles = len(_WPREP_PLAN)
    last_blk = (_PROJ_W - 1) // LANES

    def src_spec(u):
        return pl.BlockSpec((None, D, LANES), lambda t, tbl: (layer, 0, jnp.minimum(tbl[0, t] + u, last_blk)))

    return pl.pallas_call(
        _wprep_kernel,
        grid_spec=pltpu.PrefetchScalarGridSpec(
            num_scalar_prefetch=1, grid=(n_tiles,),
            in_specs=[src_spec(u) for u in range(WPREP_SRC_BLOCKS)],
            out_specs=pl.BlockSpec((D, WPREP_TN), lambda t, tbl: (0, t))),
        out_shape=jax.ShapeDtypeStruct((D, n_tiles * WPREP_TN), BF16),
        compiler_params=pltpu.CompilerParams(dimension_semantics=("arbitrary",), vmem_limit_bytes=VMEM_LIMIT),
        name="weight_relayout",
    )(tbl, *([w_in] * WPREP_SRC_BLOCKS))


def _layer(x, norm_w, w_in, layer, conv_w, a_log, dt_bias, dn_norm_w, w_o_dn, w_o_dil, w_out, final_w, final):
    B, S, D = x.shape
    T = B * S
    w_all = _wprep_call(w_in, layer)
    g1_col, g2_col, w1_col = 0, GROUP_W, 2 * GROUP_W
    w2_col = w1_col + 3 * DN_W
    small_col = w2_col + 5 * D
    ga_off = DN_W
    gb_off = ga_off + D
    zb_off = gb_off + D
    off0 = zb_off + DIL_W
    tn = D
    tiles2 = (((0, tn, "silu"),), ((0, tn, "sigmoid"),), ((0, tn, "sigmoid"),),
              ((0, DIL_W, "silu"), (DIL_W, tn, "none")), ((0, tn, "none"),))
    plain = (((0, GROUP_W, "none"),),)

    h, h4, h16, small = _norm_call(x, norm_w.astype(F32), w_all, small_col)
    qkv = _proj_conv_call(h.reshape(T, D), w_all, w1_col, 3 * DN_W, conv_w, S, "proj_conv_deltanet")
    p2 = _proj_call(h.reshape(T, D), w_all, w2_col, tn, tiles2, BF16, "proj_gates_group0")
    g1 = _proj_call(h4.reshape(T, D), w_all, g1_col, GROUP_W, plain, BF16, "proj_group1")
    g2 = _proj_call(h16.reshape(T, D), w_all, g2_col, GROUP_W, plain, BF16, "proj_group2")

    o_a = _deltanet_call(qkv, small.reshape(T, LANES), p2, a_log, dt_bias, dn_norm_w, B, S)
    o_b = _attn_call(p2, off0, zb_off, g1, g2, B, S)
    out = _out_call(o_a, o_b.reshape(T, DIL_W), p2, ga_off, gb_off, x.reshape(T, D),
                    w_o_dn.astype(BF16), w_o_dil.astype(BF16), w_out.astype(BF16),
                    final_w.astype(F32).reshape(1, D), final)
    return out.reshape(B, S, D)


def kernel(x, norm_w, w_in, conv_w, a_log, dt_bias, dn_norm_w, w_o_dn, w_o_dil, w_out, final_norm_w):
    depth = norm_w.shape[0]
    for l in range(depth):
        x = _layer(x, norm_w[l], w_in, l, conv_w[l], a_log[l], dt_bias[l], dn_norm_w[l],
                   w_o_dn[l], w_o_dil[l], w_out[l], final_norm_w, final=(l == depth - 1))
    return x
```

```python
import functools
import math

import jax
import jax.numpy as jnp
import numpy as np
from jax import lax
from jax.experimental import pallas as pl
from jax.experimental.pallas import tpu as pltpu

F32 = jnp.float32
BF16 = jnp.bfloat16

D_MODEL = 1024
DN_HEADS = 8
DN_DK = 128
DN_DV = 128
DN_CONV = 4
DN_CHUNK = 64
DIL_GROUPS = ((128, 1), (512, 4), (2048, 16))
DIL_HEADS = 4
DIL_DH = 128
ATT_BLOCK = 128
NORM_EPS = 1e-6
LOG2E = math.log2(math.e)
N_DIL = len(DIL_GROUPS)
DN_W = DN_HEADS * DN_DK
DIL_W = DIL_HEADS * DIL_DH
GROUP_W = 3 * DIL_W
PROJ_SIZES = (DN_W, DN_W, DN_W, DN_W, DN_HEADS, DN_HEADS,
              N_DIL * DIL_W, N_DIL * DIL_W, N_DIL * DIL_W, DIL_W, D_MODEL, D_MODEL)

LANES = 128
VMEM_LIMIT = 56 * 1024 * 1024

NORM_TM = 512
PROJ_TM = 2048
PROJ_TN = 1024
PROJ_SUB = 512
CONV_SUB = 512
DN_TB = 512
DN_UNIT = 2
ATT_BATCH = 8
OUT_TM = 1024


def _sigmoid(x):
    return 1.0 / (1.0 + jnp.exp2(x * (-LOG2E)))


def _silu(x):
    return x * _sigmoid(x)


def _softplus(x):
    return jnp.maximum(x, 0.0) + jnp.log1p(jnp.exp(-jnp.abs(x)))


def _mm(a, b):
    return lax.dot_general(a.astype(BF16), b.astype(BF16), (((1,), (0,)), ((), ())),
                           preferred_element_type=F32)


def _mm_nt(a, b):
    return lax.dot_general(a.astype(BF16), b.astype(BF16), (((1,), (1,)), ((), ())),
                           preferred_element_type=F32)


def _mm_tn(a, b):
    return lax.dot_general(a.astype(BF16), b.astype(BF16), (((0,), (0,)), ((), ())),
                           preferred_element_type=F32)


def _norm_kernel(x_ref, w_ref, ws_ref, h_ref, h4_ref, h16_ref, sm_ref, slab_ref):
    x = x_ref[...]
    y = x * lax.rsqrt(jnp.mean(x * x, axis=-1, keepdims=True) + NORM_EPS) * w_ref[...]
    hb = y.astype(BF16)
    h_ref[...] = hb
    sm_ref[...] = jnp.dot(hb, ws_ref[...], preferred_element_type=F32)
    n_slabs = D_MODEL // LANES
    for c in range(n_slabs):
        slab_ref[c] = y[:, c * LANES:(c + 1) * LANES]
    tm = x.shape[0]
    for d, out in ((4, h4_ref), (16, h16_ref)):
        n = tm // d
        for r in range(d):
            for c in range(n_slabs):
                out[r, :, c * LANES:(c + 1) * LANES] = slab_ref[c, pl.ds(r, n, stride=d), :].astype(BF16)


def _norm_call(x, w, w_all, small_col):
    B, S, D = x.shape
    tm = NORM_TM
    nt = S // tm
    small_blk = small_col // LANES
    return pl.pallas_call(
        _norm_kernel,
        grid=(B, nt),
        in_specs=[pl.BlockSpec((None, tm, D), lambda b, i: (b, i, 0)),
                  pl.BlockSpec((1, D), lambda b, i: (0, 0)),
                  pl.BlockSpec((D, LANES), lambda b, i: (0, small_blk))],
        out_specs=[pl.BlockSpec((None, tm, D), lambda b, i: (b, i, 0)),
                   pl.BlockSpec((None, 4, tm // 4, D), lambda b, i: (b, 0, i, 0)),
                   pl.BlockSpec((None, 16, tm // 16, D), lambda b, i: (b, 0, i, 0)),
                   pl.BlockSpec((None, tm, LANES), lambda b, i: (b, i, 0))],
        out_shape=[jax.ShapeDtypeStruct((B, S, D), BF16),
                   jax.ShapeDtypeStruct((B, 4, S // 4, D), BF16),
                   jax.ShapeDtypeStruct((B, 16, S // 16, D), BF16),
                   jax.ShapeDtypeStruct((B, S, LANES), F32)],
        scratch_shapes=[pltpu.VMEM((D // LANES, tm, LANES), F32)],
        compiler_params=pltpu.CompilerParams(dimension_semantics=("arbitrary", "arbitrary"),
                                             vmem_limit_bytes=VMEM_LIMIT),
        name="norm_deinterleave",
    )(x, w.reshape(1, D), w_all)


_ACTS = {"none": lambda v: v, "silu": _silu, "sigmoid": _sigmoid}


def _proj_kernel(h_ref, w_ref, o_ref, *, tiles):
    tm = h_ref.shape[0]
    sub = PROJ_SUB

    def run(parts):
        for m in range(tm // sub):
            rows = slice(m * sub, (m + 1) * sub)
            acc = jnp.dot(h_ref[rows, :], w_ref[...], preferred_element_type=F32)
            for lo, hi, act in parts:
                o_ref[rows, lo:hi] = _ACTS[act](acc[:, lo:hi]).astype(o_ref.dtype)

    kinds = sorted(set(tiles), key=tiles.index)
    if len(kinds) == 1:
        run(kinds[0])
        return
    j = pl.program_id(1)
    for parts in kinds:
        cond = functools.reduce(lambda a, b: a | b, [j == t for t, p in enumerate(tiles) if p == parts])
        pl.when(cond)(functools.partial(run, parts))


def _proj_call(h2d, w_all, col0, tn, tiles, out_dtype, name):
    T, D = h2d.shape
    N = len(tiles) * tn
    tm = PROJ_TM
    assert col0 % tn == 0 and T % tm == 0
    blk0 = col0 // tn
    return pl.pallas_call(
        functools.partial(_proj_kernel, tiles=tiles),
        grid=(T // tm, N // tn),
        in_specs=[pl.BlockSpec((tm, D), lambda i, j: (i, 0)),
                  pl.BlockSpec((D, tn), lambda i, j: (0, blk0 + j))],
        out_specs=pl.BlockSpec((tm, tn), lambda i, j: (i, j)),
        out_shape=jax.ShapeDtypeStruct((T, N), out_dtype),
        compiler_params=pltpu.CompilerParams(dimension_semantics=("arbitrary", "arbitrary"),
                                             vmem_limit_bytes=VMEM_LIMIT),
        name=name,
    )(h2d, w_all)


def _proj_conv_kernel(h_ref, w_ref, cw_ref, o_ref, u_ref, y_ref, *, tiles_per_seq):
    i = pl.program_id(1)
    tm = h_ref.shape[0]
    sub = CONV_SUB
    half = sub // 2
    n_slab = u_ref.shape[0]

    @pl.when(i % tiles_per_seq == 0)
    def _():
        u_ref[:, 0:8, :] = jnp.zeros((n_slab, 8, LANES), F32)

    @pl.when(i % tiles_per_seq != 0)
    def _():
        u_ref[:, 0:8, :] = u_ref[:, tm:tm + 8, :]

    for m in range(tm // sub):
        base = 8 + m * sub
        acc = jnp.dot(h_ref[m * sub:(m + 1) * sub, :], w_ref[...], preferred_element_type=F32)
        for c in range(n_slab):
            u_ref[c, base:base + sub, :] = acc[:, c * LANES:(c + 1) * LANES]
        for c in range(n_slab):
            taps = [cw_ref[k:k + 1, c * LANES:(c + 1) * LANES] for k in range(DN_CONV)]
            u = [u_ref[c, pl.ds(base - (DN_CONV - 1) + k, half, stride=2), :] for k in range(DN_CONV + 1)]
            for phase in range(2):
                y = taps[0] * u[phase]
                for k in range(1, DN_CONV):
                    y = y + taps[k] * u[phase + k]
                y_ref[c, pl.ds(phase, half, stride=2), :] = y + y * jnp.tanh(y)
            o_ref[m * sub:(m + 1) * sub, c * LANES:(c + 1) * LANES] = y_ref[c].astype(o_ref.dtype)


def _proj_conv_call(h2d, w_all, col0, N, conv_w, S, name):
    T, D = h2d.shape
    tm, tn = PROJ_TM, PROJ_TN
    assert N % tn == 0 and col0 % tn == 0 and T % tm == 0 and S % tm == 0
    blk0 = col0 // tn
    cw = jnp.zeros((8, N), F32).at[0:DN_CONV].set(0.5 * conv_w.astype(F32))
    return pl.pallas_call(
        functools.partial(_proj_conv_kernel, tiles_per_seq=S // tm),
        grid=(N // tn, T // tm),
        in_specs=[pl.BlockSpec((tm, D), lambda j, i: (i, 0)),
                  pl.BlockSpec((D, tn), lambda j, i: (0, blk0 + j)),
                  pl.BlockSpec((8, tn), lambda j, i: (0, j))],
        out_specs=pl.BlockSpec((tm, tn), lambda j, i: (i, j)),
        out_shape=jax.ShapeDtypeStruct((T, N), BF16),
        scratch_shapes=[pltpu.VMEM((tn // LANES, tm + 8, LANES), F32),
                        pltpu.VMEM((tn // LANES, CONV_SUB, LANES), F32)],
        compiler_params=pltpu.CompilerParams(dimension_semantics=("arbitrary", "arbitrary"),
                                             vmem_limit_bytes=VMEM_LIMIT),
        name=name,
    )(h2d, w_all, cw)


def _deltanet_kernel(qkv_ref, sm_ref, z_ref, hp_ref, nw_ref, o_ref, st_ref):
    C = DN_CHUNK
    tb = o_ref.shape[0]

    @pl.when(pl.program_id(1) == 0)
    def _():
        st_ref[...] = jnp.zeros_like(st_ref)

    row = lax.broadcasted_iota(jnp.int32, (C, C), 0)
    col = lax.broadcasted_iota(jnp.int32, (C, C), 1)
    causal = row >= col
    strict = row > col
    eye =(row == col).astype(F32)
    hp = hp_ref[...]
    neg_a = -jnp.exp(hp[0:1, :])
    dt_b = hp[1:2, :]
    nw = nw_ref[...]

    heads = range(DN_HEADS)

    def bcast(a, lane):
        return jnp.broadcast_to(a[:, lane:lane + 1], (C, LANES))

    def load(base, c0):
        return qkv_ref[base:base + C, c0:c0 + LANES].astype(F32)

    row_l = lax.broadcasted_iota(jnp.int32, (C, LANES), 0)

    def cumsum_rows(v):
        shift = 1
        while shift < C:
            v = v + jnp.where(row_l >= shift, pltpu.roll(v, shift, axis=0), 0.0)
            shift *= 2
        return v

    def prologue(unit, out):
        dec = {}
        for c in unit:
            sm = sm_ref[c * C:(c + 1) * C, :]
            g_all = neg_a * _softplus(sm + dt_b)
            gc_all = cumsum_rows(g_all)
            gc_t = jnp.concatenate([gc_all, jnp.zeros_like(gc_all)], axis=0).T
            glast_all = gc_all[C - 1:C, :]
            dec[c] = dict(beta=_sigmoid(sm), gc=gc_all, gc_t=gc_t, eg=jnp.exp(gc_all),
                          ek=jnp.exp(glast_all - gc_all), dl=jnp.exp(glast_all))
            yield
        q, k, kb, eg, gamma, rhs = [], [], [], [], [], []
        for c in unit:
            for h in heads:
                lane = DN_HEADS + h
                base = c * C
                qh = load(base, h * DN_DK)
                kh = load(base, DN_W + h * DN_DK)
                vh = load(base, 2 * DN_W + h * DN_DV)
                qh = qh * (lax.rsqrt(jnp.sum(qh * qh, axis=-1, keepdims=True) + NORM_EPS) * (DN_DK ** -0.5))
                kh = kh * lax.rsqrt(jnp.sum(kh * kh, axis=-1, keepdims=True) + NORM_EPS)
                beta = bcast(dec[c]["beta"], h)
                egh = bcast(dec[c]["eg"], lane)
                kbh = kh * beta
                q.append(qh)
                k.append(kh)
                kb.append(kbh)
                eg.append(egh)
                rhs.append(jnp.concatenate([vh * beta, kbh * egh], axis=1))
                diff = bcast(dec[c]["gc"], lane)[:, 0:C] - dec[c]["gc_t"][lane:lane + 1, 0:C]
                gamma.append(jnp.exp(jnp.where(causal, diff, -jnp.inf)))
                if h % 2 == 1:
                    yield
        out.update(dec=dec, q=q, k=k, kb=kb, eg=eg, gamma=gamma, rhs=rhs)

    def solve(unit, pro, out):
        idx = range(len(unit) * DN_HEADS)
        q, k, kb, gamma, rhs = pro["q"], pro["k"], pro["kb"], pro["gamma"], pro["rhs"]
        a = [_mm_nt(jnp.concatenate([kb[i], q[i]], axis=0), k[i]) for i in idx]
        yield
        p = [jnp.where(strict, -(a[i][0:C] * gamma[i]), 0.0) for i in idx]
        aqk = [a[i][C:] * gamma[i] for i in idx]
        t = [eye + p[i] for i in idx]
        p = [_mm(p[i], p[i]) for i in idx]
        yield
        for _ in range(4):
            r = [_mm(jnp.concatenate([p[i], t[i]], axis=0), p[i]) for i in idx]
            t = [t[i] + r[i][C:] for i in idx]
            p = [r[i][0:C] for i in idx]
            yield
        t = [t[i] + _mm(t[i], p[i]) for i in idx]
        yield
        out.update(x=[_mm(t[i], rhs[i]) for i in idx], aqk=aqk)

    state = [st_ref[h] for h in heads]

    def recur(unit, pro, sol):
        for n, c in enumerate(unit):
            base = c * C
            at = lambda lst, h: lst[n * DN_HEADS + h]
            ws_qs = [_mm(jnp.concatenate([at(sol["x"], h)[:, DN_DV:], at(pro["q"], h) * at(pro["eg"], h)], axis=0),
                         state[h]) for h in heads]
            yield
            v_new = [at(sol["x"], h)[:, 0:DN_DV] - ws_qs[h][0:C] for h in heads]
            o = [ws_qs[h][C:] + _mm(at(sol["aqk"], h), v_new[h]) for h in heads]
            for h in heads:
                lane = DN_HEADS + h
                kd = at(pro["k"], h) * bcast(pro["dec"][c]["ek"], lane)
                dl = jnp.broadcast_to(pro["dec"][c]["dl"][:, lane:lane + 1], (DN_DK, DN_DV))
                state[h] = state[h] * dl + _mm_tn(kd, v_new[h])
            yield
            for h in heads:
                y = o[h] * lax.rsqrt(jnp.mean(o[h] * o[h], axis=-1, keepdims=True) + NORM_EPS) * nw
                gate = z_ref[base:base + C, h * DN_DV:(h + 1) * DN_DV].astype(F32)
                o_ref[base:base + C, h * DN_DV:(h + 1) * DN_DV] = (y * gate).astype(o_ref.dtype)
                if h % 4 == 3:
                    yield

    def interleave(gens):
        gens = list(gens)
        while gens:
            for g in list(gens):
                try:
                    next(g)
                except StopIteration:
                    gens.remove(g)

    n_chunks = tb // C
    units = [list(range(u, min(u + DN_UNIT, n_chunks))) for u in range(0, n_chunks, DN_UNIT)]
    pro = [dict() for _ in units]
    sol = [dict() for _ in units]
    for tick in range(len(units) + 2):
        active = []
        if 0 <= tick - 2 < len(units):
            active.append(recur(units[tick - 2], pro[tick - 2], sol[tick - 2]))
        if 0 <= tick - 1 < len(units):
            active.append(solve(units[tick - 1], pro[tick - 1], sol[tick - 1]))
        if tick < len(units):
            active.append(prologue(units[tick], pro[tick]))
        interleave(active)
    for h in heads:
        st_ref[h] = state[h]


def _deltanet_call(qkv, small, p2, a_log, dt_bias, dn_norm_w, B, S):
    tb = DN_TB
    nt = S // tb
    hp = jnp.zeros((8, LANES), F32)
    hp = hp.at[0, DN_HEADS:2 * DN_HEADS].set(a_log.astype(F32)).at[1, DN_HEADS:2 * DN_HEADS].set(dt_bias.astype(F32))
    nw = dn_norm_w.astype(F32).reshape(1, DN_DV)
    return pl.pallas_call(
        _deltanet_kernel,
        grid=(B, nt),
        in_specs=[pl.BlockSpec((tb, 3 * DN_W), lambda b, i: (b * nt + i, 0)),
                  pl.BlockSpec((tb, LANES), lambda b, i: (b * nt + i, 0)),
                  pl.BlockSpec((tb, DN_W), lambda b, i: (b * nt + i, 0)),
                  pl.BlockSpec((8, LANES), lambda b, i: (0, 0)),
                  pl.BlockSpec((1, DN_DV), lambda b, i: (0, 0))],
        out_specs=pl.BlockSpec((tb, DN_W), lambda b, i: (b * nt + i, 0)),
        out_shape=jax.ShapeDtypeStruct((B * S, DN_W), BF16),
        scratch_shapes=[pltpu.VMEM((DN_HEADS, DN_DK, DN_DV), F32)],
        compiler_params=pltpu.CompilerParams(dimension_semantics=("arbitrary", "arbitrary"),
                                             vmem_limit_bytes=VMEM_LIMIT),
        name="deltanet",
    )(qkv, small, p2, hp, nw)


def _alibi_slope(idx):
    n = N_DIL * DIL_HEADS
    return 2.0 ** (-8.0 * (idx + 1) / n)


def _attn_kernel(q0_ref, k0_ref, v0_ref, q1_ref, k1_ref, v1_ref, q2_ref, k2_ref, v2_ref, z_ref, o_ref,
                 num_ref, den_ref, mx_ref):
    Q = ATT_BLOCK
    hd = pl.program_id(1)
    span = DIL_GROUPS[0][0] // DIL_GROUPS[0][1]
    dist = (Q + lax.broadcasted_iota(jnp.int32, (Q, 2 * Q), 0) - lax.broadcasted_iota(jnp.int32, (Q, 2 * Q), 1))
    valid = (dist >= 0) & (dist <= span)
    distf = dist.astype(F32)

    def slope_of(g):
        s = jnp.float32(_alibi_slope(g * DIL_HEADS))
        for h in range(1, DIL_HEADS):
            s = jnp.where(hd == h, jnp.float32(_alibi_slope(g * DIL_HEADS + h)), s)
        return s

    def group_batches(g, q_ref, k_ref, v_ref, emit):
        d = DIL_GROUPS[g][1]
        bias2 = jnp.where(valid, -(slope_of(g) * (float(d) * LOG2E)) * distf, -jnp.inf)
        bias1 = bias2[:, Q:]
        nb = q_ref.shape[-2] // Q

        def item(r, j):
            qr, kr, vr = (q_ref, k_ref, v_ref) if d == 1 else (q_ref.at[r], k_ref.at[r], v_ref.at[r])
            if j == 0:
                return qr[0:Q, :], kr[0:Q, :], vr[0:Q, :], bias1
            lo, cur = (j - 1) * Q, j * Q
            return qr[cur:cur + Q, :], kr[lo:lo + 2 * Q, :], vr[lo:lo + 2 * Q, :], bias2

        def batch(ids):
            items = [item(r, j) for r, j in ids]
            s = [_mm_nt(q, k) + bias for q, k, v, bias in items]
            mx = [jnp.max(si, axis=-1, keepdims=True) for si in s]
            yield
            p = [jnp.exp2(si - mi) for si, mi in zip(s, mx)]
            den = [jnp.sum(pi, axis=-1, keepdims=True) for pi in p]
            num = [_mm(pi, it[2]) for pi, it in zip(p, items)]
            yield
            for (r, j), n, dn, m in zip(ids, num, den, mx):
                emit(r, j, (n, jnp.broadcast_to(dn, (Q, LANES)), jnp.broadcast_to(m, (Q, LANES))))

        ids = [(r, j) for r in range(d) for j in range(nb)]
        return [batch(ids[i:i + ATT_BATCH]) for i in range(0, len(ids), ATT_BATCH)]

    def rows(g, r, j):
        d = DIL_GROUPS[g][1]
        if d == 1:
            return pl.ds(j * Q, Q)
        return pl.ds(j * (Q * d) + r, Q, stride=d)

    def store_to(g, slot):
        def emit(r, j, res):
            idx = rows(g, r, j)
            num_ref[slot, idx, :] = res[0]
            den_ref[slot, idx, :] = res[1]
            mx_ref[slot, idx, :] = res[2]
        return emit

    def emit_last(r, j, res):
        idx = rows(0, r, j)
        parts = [(num_ref[s, idx, :], den_ref[s, idx, :], mx_ref[s, idx, :]) for s in range(2)] + [res]
        m = jnp.maximum(jnp.maximum(parts[0][2], parts[1][2]), parts[2][2])
        w = [jnp.exp2(p[2] - m) for p in parts]
        n = parts[0][0] * w[0] + parts[1][0] * w[1] + parts[2][0] * w[2]
        dn = parts[0][1] * w[0] + parts[1][1] * w[1] + parts[2][1] * w[2]
        o_ref[idx, :] = ((n / dn) * z_ref[idx, :].astype(F32)).astype(o_ref.dtype)

    pending = (group_batches(2, q2_ref, k2_ref, v2_ref, store_to(2, 0))
               + group_batches(1, q1_ref, k1_ref, v1_ref, store_to(1, 1))
               + group_batches(0, q0_ref, k0_ref, v0_ref, emit_last))
    active = []
    while pending or active:
        if pending:
            active.append(pending.pop(0))
        for gen in list(active):
            try:
                next(gen)
            except StopIteration:
                active.remove(gen)


def _attn_call(p2, off0, zb_off, g1, g2, B, S):
    H = DIL_HEADS
    p2v = p2.reshape(B, S, p2.shape[1])
    g1v = g1.reshape(B, 4, S // 4, GROUP_W)
    g2v = g2.reshape(B, 16, S // 16, GROUP_W)
    b0 = off0 // LANES
    zb0 = zb_off // LANES

    def nat(cb):
        return pl.BlockSpec((None, S, LANES), lambda b, h: (b, 0, cb + h))

    def perm(d, cb):
        return pl.BlockSpec((None, d, S // d, LANES), lambda b, h: (b, 0, 0, cb + h))

    return pl.pallas_call(
        _attn_kernel,
        grid=(B, H),
        in_specs=[nat(b0), nat(b0 + H), nat(b0 + 2 * H),
                  perm(4, 0), perm(4, H), perm(4, 2 * H),
                  perm(16, 0), perm(16, H), perm(16, 2 * H),
                  nat(zb0)],
        out_specs=pl.BlockSpec((None, S, LANES), lambda b, h: (b, 0, h)),
        out_shape=jax.ShapeDtypeStruct((B, S, DIL_W), BF16),
        scratch_shapes=[pltpu.VMEM((2, S, LANES), F32)] * 3,
        compiler_params=pltpu.CompilerParams(dimension_semantics=("arbitrary", "arbitrary"),
                                             vmem_limit_bytes=VMEM_LIMIT),
        name="dilated_attention",
    )(p2v, p2v, p2v, g1v, g1v, g1v, g2v, g2v, g2v, p2v)


def _out_kernel(oa_ref, ob_ref, ga_ref, gb_ref, x_ref, wa_ref, wb_ref, wo_ref, fw_ref, o_ref, *, final):
    ya = jnp.dot(oa_ref[...], wa_ref[...], preferred_element_type=F32)
    yb = jnp.dot(ob_ref[...], wb_ref[...], preferred_element_type=F32)
    merged = ga_ref[...].astype(F32) * ya + gb_ref[...].astype(F32) * yb
    xn = x_ref[...] + jnp.dot(merged.astype(BF16), wo_ref[...], preferred_element_type=F32)
    if final:
        xn = xn * lax.rsqrt(jnp.mean(xn * xn, axis=-1, keepdims=True) + NORM_EPS) * fw_ref[...]
    o_ref[...] = xn


def _out_call(oa, ob, p2, ga_off, gb_off, x2d, wa, wb, wo, fw, final):
    T, D = x2d.shape
    tm = OUT_TM
    ga_b = ga_off // D
    gb_b = gb_off // D
    const = lambda i: (0, 0)
    return pl.pallas_call(
        functools.partial(_out_kernel, final=final),
        grid=(T // tm,),
        in_specs=[pl.BlockSpec((tm, DN_W), lambda i: (i, 0)),
                  pl.BlockSpec((tm, DIL_W), lambda i: (i, 0)),
                  pl.BlockSpec((tm, D), lambda i: (i, ga_b)),
                  pl.BlockSpec((tm, D), lambda i: (i, gb_b)),
                  pl.BlockSpec((tm, D), lambda i: (i, 0)),
                  pl.BlockSpec((DN_W, D), const),
                  pl.BlockSpec((DIL_W, D), const),
                  pl.BlockSpec((D, D), const),
                  pl.BlockSpec((1, D), const)],
        out_specs=pl.BlockSpec((tm, D), lambda i: (i, 0)),
        out_shape=jax.ShapeDtypeStruct((T, D), F32),
        compiler_params=pltpu.CompilerParams(dimension_semantics=("arbitrary",),
                                             vmem_limit_bytes=VMEM_LIMIT),
        name="out_merge",
    )(oa, ob, p2, p2, x2d, wa, wb, wo, fw)


WPREP_TN = 512
WPREP_SRC_BLOCKS = WPREP_TN // LANES + 1


def _wprep_kernel(tbl_ref, *refs):
    src_refs, o_ref = refs[:WPREP_SRC_BLOCKS], refs[WPREP_SRC_BLOCKS]
    t = pl.program_id(0)
    shift = tbl_ref[1, t]
    kind = tbl_ref[2, t]
    x = jnp.concatenate([r[...] for r in src_refs], axis=1)
    lane = lax.broadcasted_iota(jnp.int32, (1, WPREP_TN), 1)
    factor = jnp.where(kind == 1, jnp.float32(DIL_DH ** -0.5 * LOG2E), jnp.float32(1.0))
    keep = jnp.where((kind != 2) | (lane < 2 * DN_HEADS), factor, 0.0)
    for off in WPREP_SHIFTS:
        @pl.when(shift == off)
        def _(off=off):
            o_ref[...] = (x[:, off:off + WPREP_TN] * keep).astype(o_ref.dtype)


def _wprep_plan():
    starts, acc = {}, 0
    for name, size in zip(("q_a", "k_a", "v_a", "z_a", "b_a", "a_a", "q_b", "k_b", "v_b", "z_b", "g_a", "g_b"), PROJ_SIZES):
        starts[name] = acc
        acc += size

    def group(g):
        return [(starts[n] + g * DIL_W, 1 if n == "q_b" else 0) for n in ("q_b", "k_b", "v_b")]

    def span(name, width):
        return [(starts[name] + o, 0) for o in range(0, width, WPREP_TN)]

    plan = (group(1) + group(2) + span("q_a", DN_W) + span("k_a", DN_W) + span("v_a", DN_W) + span("z_a", DN_W)
            + span("g_a", D_MODEL) + span("g_b", D_MODEL) + span("z_b", DIL_W) + group(0) + [(starts["b_a"], 2)])
    return plan, acc


_WPREP_PLAN, _PROJ_W = _wprep_plan()
WPREP_SHIFTS = tuple(sorted({src % LANES for src, _ in _WPREP_PLAN}))


def _wprep_call(w_in, layer):
    D = w_in.shape[1]
    assert w_in.shape[2] == _PROJ_W
    tbl = jnp.asarray(np.array([[s // LANES for s, _ in _WPREP_PLAN], [s % LANES for s, _ in _WPREP_PLAN],
                                [k for _, k in _WPREP_PLAN]], dtype=np.int32))
    n_tiles = len(_WPREP_PLAN)
    last_blk = (_PROJ_W - 1) // LANES

    def src_spec(u):
        return pl.BlockSpec((None, D, LANES), lambda t, tbl: (layer, 0, jnp.minimum(tbl[0, t] + u, last_blk)))

    return pl.pallas_call(
        _wprep_kernel,
        grid_spec=pltpu.PrefetchScalarGridSpec(
            num_scalar_prefetch=1, grid=(n_tiles,),
            in_specs=[src_spec(u) for u in range(WPREP_SRC_BLOCKS)],
            out_specs=pl.BlockSpec((D, WPREP_TN), lambda t, tbl: (0, t))),
        out_shape=jax.ShapeDtypeStruct((D, n_tiles * WPREP_TN), BF16),
        compiler_params=pltpu.CompilerParams(dimension_semantics=("arbitrary",), vmem_limit_bytes=VMEM_LIMIT),
        name="weight_relayout",
    )(tbl, *([w_in] * WPREP_SRC_BLOCKS))


def _layer(x, norm_w, w_in, layer, conv_w, a_log, dt_bias, dn_norm_w, w_o_dn, w_o_dil, w_out, final_w, final):
    B, S, D = x.shape
    T = B * S
    w_all = _wprep_call(w_in, layer)
    g1_col, g2_col, w1_col = 0, GROUP_W, 2 * GROUP_W
    w2_col = w1_col + 3 * DN_W
    small_col = w2_col + 5 * D
    ga_off = DN_W
    gb_off = ga_off + D
    zb_off = gb_off + D
    off0 = zb_off + DIL_W
    tn = D
    tiles2 = (((0, tn, "silu"),), ((0, tn, "sigmoid"),), ((0, tn, "sigmoid"),),
              ((0, DIL_W, "silu"), (DIL_W, tn, "none")), ((0, tn, "none"),))
    plain = (((0, GROUP_W, "none"),),)

    h, h4, h16, small = _norm_call(x, norm_w.astype(F32), w_all, small_col)
    qkv = _proj_conv_call(h.reshape(T, D), w_all, w1_col, 3 * DN_W, conv_w, S, "proj_conv_deltanet")
    p2 = _proj_call(h.reshape(T, D), w_all, w2_col, tn, tiles2, BF16, "proj_gates_group0")
    g1 = _proj_call(h4.reshape(T, D), w_all, g1_col, GROUP_W, plain, BF16, "proj_group1")
    g2 = _proj_call(h16.reshape(T, D), w_all, g2_col, GROUP_W, plain, BF16, "proj_group2")

    o_a = _deltanet_call(qkv, small.reshape(T, LANES), p2, a_log, dt_bias, dn_norm_w, B, S)
    o_b = _attn_call(p2, off0, zb_off, g1, g2, B, S)
    out = _out_call(o_a, o_b.reshape(T, DIL_W), p2, ga_off, gb_off, x.reshape(T, D),
                    w_o_dn.astype(BF16), w_o_dil.astype(BF16), w_out.astype(BF16),
                    final_w.astype(F32).reshape(1, D), final)
    return out.reshape(B, S, D)


def kernel(x, norm_w, w_in, conv_w, a_log, dt_bias, dn_norm_w, w_o_dn, w_o_dil, w_out, final_norm_w):
    depth = norm_w.shape[0]
    for l in range(depth):
        x = _layer(x, norm_w[l], w_in, l, conv_w[l], a_log[l], dt_bias[l], dn_norm_w[l],
                   w_o_dn[l], w_o_dil[l], w_out[l], final_norm_w, final=(l == depth - 1))
    return x
```

```python
import functools
import math

import jax
import jax.numpy as jnp
import numpy as np
from jax import lax
from jax.experimental import pallas as pl
from jax.experimental.pallas import tpu as pltpu

F32 = jnp.float32
BF16 = jnp.bfloat16

D_MODEL = 1024
DN_HEADS = 8
DN_DK = 128
DN_DV = 128
DN_CONV = 4
DN_CHUNK = 64
DIL_GROUPS = ((128, 1), (512, 4), (2048, 16))
DIL_HEADS = 4
DIL_DH = 128
ATT_BLOCK = 128
NORM_EPS = 1e-6
LOG2E = math.log2(math.e)
N_DIL = len(DIL_GROUPS)
DN_W = DN_HEADS * DN_DK
DIL_W = DIL_HEADS * DIL_DH
GROUP_W = 3 * DIL_W
PROJ_SIZES = (DN_W, DN_W, DN_W, DN_W, DN_HEADS, DN_HEADS,
              N_DIL * DIL_W, N_DIL * DIL_W, N_DIL * DIL_W, DIL_W, D_MODEL, D_MODEL)

LANES = 128
VMEM_LIMIT = 56 * 1024 * 1024

NORM_TM = 1024
PROJ_TM = 2048
PROJ_TN = 1024
PROJ_SUB = 512
CONV_SUB = 512
DN_TB = 512
DN_UNIT = 2
ATT_BATCH = 8
OUT_TM = 1024


def _sigmoid(x):
    return 1.0 / (1.0 + jnp.exp2(x * (-LOG2E)))


def _silu(x):
    return x * _sigmoid(x)


def _softplus(x):
    return jnp.maximum(x, 0.0) + jnp.log1p(jnp.exp(-jnp.abs(x)))


def _mm(a, b):
    return lax.dot_general(a.astype(BF16), b.astype(BF16), (((1,), (0,)), ((), ())),
                           preferred_element_type=F32)


def _mm_nt(a, b):
    return lax.dot_general(a.astype(BF16), b.astype(BF16), (((1,), (1,)), ((), ())),
                           preferred_element_type=F32)


def _mm_tn(a, b):
    return lax.dot_general(a.astype(BF16), b.astype(BF16), (((0,), (0,)), ((), ())),
                           preferred_element_type=F32)


def _norm_kernel(x_ref, w_ref, ws_ref, h_ref, h4_ref, h16_ref, sm_ref, slab_ref):
    x = x_ref[...]
    y = x * lax.rsqrt(jnp.mean(x * x, axis=-1, keepdims=True) + NORM_EPS) * w_ref[...]
    hb = y.astype(BF16)
    h_ref[...] = hb
    sm_ref[...] = jnp.dot(hb, ws_ref[...], preferred_element_type=F32)
    n_slabs = D_MODEL // LANES
    for c in range(n_slabs):
        slab_ref[c] = y[:, c * LANES:(c + 1) * LANES]
    tm = x.shape[0]
    for d, out in ((4, h4_ref), (16, h16_ref)):
        n = tm // d
        for r in range(d):
            for c in range(n_slabs):
                out[r, :, c * LANES:(c + 1) * LANES] = slab_ref[c, pl.ds(r, n, stride=d), :].astype(BF16)


def _norm_call(x, w, w_all, small_col):
    B, S, D = x.shape
    tm = NORM_TM
    nt = S // tm
    small_blk = small_col // LANES
    return pl.pallas_call(
        _norm_kernel,
        grid=(B, nt),
        in_specs=[pl.BlockSpec((None, tm, D), lambda b, i: (b, i, 0)),
                  pl.BlockSpec((1, D), lambda b, i: (0, 0)),
                  pl.BlockSpec((D, LANES), lambda b, i: (0, small_blk))],
        out_specs=[pl.BlockSpec((None, tm, D), lambda b, i: (b, i, 0)),
                   pl.BlockSpec((None, 4, tm // 4, D), lambda b, i: (b, 0, i, 0)),
                   pl.BlockSpec((None, 16, tm // 16, D), lambda b, i: (b, 0, i, 0)),
                   pl.BlockSpec((None, tm, LANES), lambda b, i: (b, i, 0))],
        out_shape=[jax.ShapeDtypeStruct((B, S, D), BF16),
                   jax.ShapeDtypeStruct((B, 4, S // 4, D), BF16),
                   jax.ShapeDtypeStruct((B, 16, S // 16, D), BF16),
                   jax.ShapeDtypeStruct((B, S, LANES), F32)],
        scratch_shapes=[pltpu.VMEM((D // LANES, tm, LANES), F32)],
        compiler_params=pltpu.CompilerParams(dimension_semantics=("arbitrary", "arbitrary"),
                                             vmem_limit_bytes=VMEM_LIMIT),
        name="norm_deinterleave",
    )(x, w.reshape(1, D), w_all)


_ACTS = {"none": lambda v: v, "silu": _silu, "sigmoid": _sigmoid}


def _proj_kernel(h_ref, w_ref, o_ref, *, tiles):
    tm = h_ref.shape[0]
    sub = PROJ_SUB

    def run(parts):
        for m in range(tm // sub):
            rows = slice(m * sub, (m + 1) * sub)
            acc = jnp.dot(h_ref[rows, :], w_ref[...], preferred_element_type=F32)
            for lo, hi, act in parts:
                o_ref[rows, lo:hi] = _ACTS[act](acc[:, lo:hi]).astype(o_ref.dtype)

    kinds = sorted(set(tiles), key=tiles.index)
    if len(kinds) == 1:
        run(kinds[0])
        return
    j = pl.program_id(1)
    for parts in kinds:
        cond = functools.reduce(lambda a, b: a | b, [j == t for t, p in enumerate(tiles) if p == parts])
        pl.when(cond)(functools.partial(run, parts))


def _proj_call(h2d, w_all, col0, tn, tiles, out_dtype, name):
    T, D = h2d.shape
    N = len(tiles) * tn
    tm = PROJ_TM
    assert col0 % tn == 0 and T % tm == 0
    blk0 = col0 // tn
    return pl.pallas_call(
        functools.partial(_proj_kernel, tiles=tiles),
        grid=(T // tm, N // tn),
        in_specs=[pl.BlockSpec((tm, D), lambda i, j: (i, 0)),
                  pl.BlockSpec((D, tn), lambda i, j: (0, blk0 + j))],
        out_specs=pl.BlockSpec((tm, tn), lambda i, j: (i, j)),
        out_shape=jax.ShapeDtypeStruct((T, N), out_dtype),
        compiler_params=pltpu.CompilerParams(dimension_semantics=("arbitrary", "arbitrary"),
                                             vmem_limit_bytes=VMEM_LIMIT),
        name=name,
    )(h2d, w_all)


def _proj_conv_kernel(h_ref, w_ref, cw_ref, o_ref, u_ref, y_ref, *, tiles_per_seq):
    i = pl.program_id(1)
    tm = h_ref.shape[0]
    sub = CONV_SUB
    half = sub // 2
    n_slab = u_ref.shape[0]

    @pl.when(i % tiles_per_seq == 0)
    def _():
        u_ref[:, 0:8, :] = jnp.zeros((n_slab, 8, LANES), F32)

    @pl.when(i % tiles_per_seq != 0)
    def _():
        u_ref[:, 0:8, :] = u_ref[:, tm:tm + 8, :]

    for m in range(tm // sub):
        base = 8 + m * sub
        acc = jnp.dot(h_ref[m * sub:(m + 1) * sub, :], w_ref[...], preferred_element_type=F32)
        for c in range(n_slab):
            u_ref[c, base:base + sub, :] = acc[:, c * LANES:(c + 1) * LANES]
        for c in range(n_slab):
            taps = [cw_ref[k:k + 1, c * LANES:(c + 1) * LANES] for k in range(DN_CONV)]
            u = [u_ref[c, pl.ds(base - (DN_CONV - 1) + k, half, stride=2), :] for k in range(DN_CONV + 1)]
            for phase in range(2):
                y = taps[0] * u[phase]
                for k in range(1, DN_CONV):
                    y = y + taps[k] * u[phase + k]
                y_ref[c, pl.ds(phase, half, stride=2), :] = y + y * jnp.tanh(y)
            o_ref[m * sub:(m + 1) * sub, c * LANES:(c + 1) * LANES] = y_ref[c].astype(o_ref.dtype)


def _proj_conv_call(h2d, w_all, col0, N, conv_w, S, name):
    T, D = h2d.shape
    tm, tn = PROJ_TM, PROJ_TN
    assert N % tn == 0 and col0 % tn == 0 and T % tm == 0 and S % tm == 0
    blk0 = col0 // tn
    cw = jnp.zeros((8, N), F32).at[0:DN_CONV].set(0.5 * conv_w.astype(F32))
    return pl.pallas_call(
        functools.partial(_proj_conv_kernel, tiles_per_seq=S // tm),
        grid=(N // tn, T // tm),
        in_specs=[pl.BlockSpec((tm, D), lambda j, i: (i, 0)),
                  pl.BlockSpec((D, tn), lambda j, i: (0, blk0 + j)),
                  pl.BlockSpec((8, tn), lambda j, i: (0, j))],
        out_specs=pl.BlockSpec((tm, tn), lambda j, i: (i, j)),
        out_shape=jax.ShapeDtypeStruct((T, N), BF16),
        scratch_shapes=[pltpu.VMEM((tn // LANES, tm + 8, LANES), F32),
                        pltpu.VMEM((tn // LANES, CONV_SUB, LANES), F32)],
        compiler_params=pltpu.CompilerParams(dimension_semantics=("arbitrary", "arbitrary"),
                                             vmem_limit_bytes=VMEM_LIMIT),
        name=name,
    )(h2d, w_all, cw)


def _deltanet_kernel(qkv_ref, sm_ref, z_ref, hp_ref, nw_ref, o_ref, st_ref):
    C = DN_CHUNK
    tb = o_ref.shape[0]

    @pl.when(pl.program_id(1) == 0)
    def _():
        st_ref[...] = jnp.zeros_like(st_ref)

    row = lax.broadcasted_iota(jnp.int32, (C, C), 0)
    col = lax.broadcasted_iota(jnp.int32, (C, C), 1)
    causal = row >= col
    strict = row > col
    eye =(row == col).astype(F32)
    hp = hp_ref[...]
    neg_a = -jnp.exp(hp[0:1, :])
    dt_b = hp[1:2, :]
    nw = nw_ref[...]

    heads = range(DN_HEADS)

    def bcast(a, lane):
        return jnp.broadcast_to(a[:, lane:lane + 1], (C, LANES))

    def load(base, c0):
        return qkv_ref[base:base + C, c0:c0 + LANES].astype(F32)

    row_l = lax.broadcasted_iota(jnp.int32, (C, LANES), 0)

    def cumsum_rows(v):
        shift = 1
        while shift < C:
            v = v + jnp.where(row_l >= shift, pltpu.roll(v, shift, axis=0), 0.0)
            shift *= 2
        return v

    def prologue(unit, out):
        dec = {}
        for c in unit:
            sm = sm_ref[c * C:(c + 1) * C, :]
            g_all = neg_a * _softplus(sm + dt_b)
            gc_all = cumsum_rows(g_all)
            gc_t = jnp.concatenate([gc_all, jnp.zeros_like(gc_all)], axis=0).T
            glast_all = gc_all[C - 1:C, :]
            dec[c] = dict(beta=_sigmoid(sm), gc=gc_all, gc_t=gc_t, eg=jnp.exp(gc_all),
                          ek=jnp.exp(glast_all - gc_all), dl=jnp.exp(glast_all))
            yield
        q, k, kb, eg, gamma, rhs = [], [], [], [], [], []
        for c in unit:
            for h in heads:
                lane = DN_HEADS + h
                base = c * C
                qh = load(base, h * DN_DK)
                kh = load(base, DN_W + h * DN_DK)
                vh = load(base, 2 * DN_W + h * DN_DV)
                qh = qh * (lax.rsqrt(jnp.sum(qh * qh, axis=-1, keepdims=True) + NORM_EPS) * (DN_DK ** -0.5))
                kh = kh * lax.rsqrt(jnp.sum(kh * kh, axis=-1, keepdims=True) + NORM_EPS)
                beta = bcast(dec[c]["beta"], h)
                egh = bcast(dec[c]["eg"], lane)
                kbh = kh * beta
                q.append(qh)
                k.append(kh)
                kb.append(kbh)
                eg.append(egh)
                rhs.append(jnp.concatenate([vh * beta, kbh * egh], axis=1))
                diff = bcast(dec[c]["gc"], lane)[:, 0:C] - dec[c]["gc_t"][lane:lane + 1, 0:C]
                gamma.append(jnp.exp(jnp.where(causal, diff, -jnp.inf)))
                if h % 2 == 1:
                    yield
        out.update(dec=dec, q=q, k=k, kb=kb, eg=eg, gamma=gamma, rhs=rhs)

    def solve(unit, pro, out):
        idx = range(len(unit) * DN_HEADS)
        q, k, kb, gamma, rhs = pro["q"], pro["k"], pro["kb"], pro["gamma"], pro["rhs"]
        a = [_mm_nt(jnp.concatenate([kb[i], q[i]], axis=0), k[i]) for i in idx]
        yield
        p = [jnp.where(strict, -(a[i][0:C] * gamma[i]), 0.0) for i in idx]
        aqk = [a[i][C:] * gamma[i] for i in idx]
        t = [eye + p[i] for i in idx]
        p = [_mm(p[i], p[i]) for i in idx]
        yield
        for _ in range(4):
            r = [_mm(jnp.concatenate([p[i], t[i]], axis=0), p[i]) for i in idx]
            t = [t[i] + r[i][C:] for i in idx]
            p = [r[i][0:C] for i in idx]
            yield
        t = [t[i] + _mm(t[i], p[i]) for i in idx]
        yield
        out.update(x=[_mm(t[i], rhs[i]) for i in idx], aqk=aqk)

    state = [st_ref[h] for h in heads]

    def recur(unit, pro, sol):
        for n, c in enumerate(unit):
            base = c * C
            at = lambda lst, h: lst[n * DN_HEADS + h]
            ws_qs = [_mm(jnp.concatenate([at(sol["x"], h)[:, DN_DV:], at(pro["q"], h) * at(pro["eg"], h)], axis=0),
                         state[h]) for h in heads]
            yield
            v_new = [at(sol["x"], h)[:, 0:DN_DV] - ws_qs[h][0:C] for h in heads]
            o = [ws_qs[h][C:] + _mm(at(sol["aqk"], h), v_new[h]) for h in heads]
            for h in heads:
                lane = DN_HEADS + h
                kd = at(pro["k"], h) * bcast(pro["dec"][c]["ek"], lane)
                dl = jnp.broadcast_to(pro["dec"][c]["dl"][:, lane:lane + 1], (DN_DK, DN_DV))
                state[h] = state[h] * dl + _mm_tn(kd, v_new[h])
            yield
            for h in heads:
                y = o[h] * lax.rsqrt(jnp.mean(o[h] * o[h], axis=-1, keepdims=True) + NORM_EPS) * nw
                gate = z_ref[base:base + C, h * DN_DV:(h + 1) * DN_DV].astype(F32)
                o_ref[base:base + C, h * DN_DV:(h + 1) * DN_DV] = (y * gate).astype(o_ref.dtype)
                if h % 4 == 3:
                    yield

    def interleave(gens):
        gens = list(gens)
        while gens:
            for g in list(gens):
                try:
                    next(g)
                except StopIteration:
                    gens.remove(g)

    n_chunks = tb // C
    units = [list(range(u, min(u + DN_UNIT, n_chunks))) for u in range(0, n_chunks, DN_UNIT)]
    pro = [dict() for _ in units]
    sol = [dict() for _ in units]
    for tick in range(len(units) + 2):
        active = []
        if 0 <= tick - 2 < len(units):
            active.append(recur(units[tick - 2], pro[tick - 2], sol[tick - 2]))
        if 0 <= tick - 1 < len(units):
            active.append(solve(units[tick - 1], pro[tick - 1], sol[tick - 1]))
        if tick < len(units):
            active.append(prologue(units[tick], pro[tick]))
        interleave(active)
    for h in heads:
        st_ref[h] = state[h]


def _deltanet_call(qkv, small, p2, a_log, dt_bias, dn_norm_w, B, S):
    tb = DN_TB
    nt = S // tb
    hp = jnp.zeros((8, LANES), F32)
    hp = hp.at[0, DN_HEADS:2 * DN_HEADS].set(a_log.astype(F32)).at[1, DN_HEADS:2 * DN_HEADS].set(dt_bias.astype(F32))
    nw = dn_norm_w.astype(F32).reshape(1, DN_DV)
    return pl.pallas_call(
        _deltanet_kernel,
        grid=(B, nt),
        in_specs=[pl.BlockSpec((tb, 3 * DN_W), lambda b, i: (b * nt + i, 0)),
                  pl.BlockSpec((tb, LANES), lambda b, i: (b * nt + i, 0)),
                  pl.BlockSpec((tb, DN_W), lambda b, i: (b * nt + i, 0)),
                  pl.BlockSpec((8, LANES), lambda b, i: (0, 0)),
                  pl.BlockSpec((1, DN_DV), lambda b, i: (0, 0))],
        out_specs=pl.BlockSpec((tb, DN_W), lambda b, i: (b * nt + i, 0)),
        out_shape=jax.ShapeDtypeStruct((B * S, DN_W), BF16),
        scratch_shapes=[pltpu.VMEM((DN_HEADS, DN_DK, DN_DV), F32)],
        compiler_params=pltpu.CompilerParams(dimension_semantics=("arbitrary", "arbitrary"),
                                             vmem_limit_bytes=VMEM_LIMIT),
        name="deltanet",
    )(qkv, small, p2, hp, nw)


def _alibi_slope(idx):
    n = N_DIL * DIL_HEADS
    return 2.0 ** (-8.0 * (idx + 1) / n)


def _attn_kernel(q0_ref, k0_ref, v0_ref, q1_ref, k1_ref, v1_ref, q2_ref, k2_ref, v2_ref, z_ref, o_ref,
                 num_ref, den_ref, mx_ref):
    Q = ATT_BLOCK
    hd = pl.program_id(1)
    span = DIL_GROUPS[0][0] // DIL_GROUPS[0][1]
    dist = (Q + lax.broadcasted_iota(jnp.int32, (Q, 2 * Q), 0) - lax.broadcasted_iota(jnp.int32, (Q, 2 * Q), 1))
    valid = (dist >= 0) & (dist <= span)
    distf = dist.astype(F32)

    def slope_of(g):
        s = jnp.float32(_alibi_slope(g * DIL_HEADS))
        for h in range(1, DIL_HEADS):
            s = jnp.where(hd == h, jnp.float32(_alibi_slope(g * DIL_HEADS + h)), s)
        return s

    def group_batches(g, q_ref, k_ref, v_ref, emit):
        d = DIL_GROUPS[g][1]
        bias2 = jnp.where(valid, -(slope_of(g) * (float(d) * LOG2E)) * distf, -jnp.inf)
        bias1 = bias2[:, Q:]
        nb = q_ref.shape[-2] // Q

        def item(r, j):
            qr, kr, vr = (q_ref, k_ref, v_ref) if d == 1 else (q_ref.at[r], k_ref.at[r], v_ref.at[r])
            if j == 0:
                return qr[0:Q, :], kr[0:Q, :], vr[0:Q, :], bias1
            lo, cur = (j - 1) * Q, j * Q
            return qr[cur:cur + Q, :], kr[lo:lo + 2 * Q, :], vr[lo:lo + 2 * Q, :], bias2

        def batch(ids):
            items = [item(r, j) for r, j in ids]
            s = [_mm_nt(q, k) + bias for q, k, v, bias in items]
            mx = [jnp.max(si, axis=-1, keepdims=True) for si in s]
            yield
            p = [jnp.exp2(si - mi) for si, mi in zip(s, mx)]
            den = [jnp.sum(pi, axis=-1, keepdims=True) for pi in p]
            num = [_mm(pi, it[2]) for pi, it in zip(p, items)]
            yield
            for (r, j), n, dn, m in zip(ids, num, den, mx):
                emit(r, j, (n, jnp.broadcast_to(dn, (Q, LANES)), jnp.broadcast_to(m, (Q, LANES))))

        ids = [(r, j) for r in range(d) for j in range(nb)]
        return [batch(ids[i:i + ATT_BATCH]) for i in range(0, len(ids), ATT_BATCH)]

    def rows(g, r, j):
        d = DIL_GROUPS[g][1]
        if d == 1:
            return pl.ds(j * Q, Q)
        return pl.ds(j * (Q * d) + r, Q, stride=d)

    def store_to(g, slot):
        def emit(r, j, res):
            idx = rows(g, r, j)
            num_ref[slot, idx, :] = res[0]
            den_ref[slot, idx, :] = res[1]
            mx_ref[slot, idx, :] = res[2]
        return emit

    def emit_last(r, j, res):
        idx = rows(0, r, j)
        parts = [(num_ref[s, idx, :], den_ref[s, idx, :], mx_ref[s, idx, :]) for s in range(2)] + [res]
        m = jnp.maximum(jnp.maximum(parts[0][2], parts[1][2]), parts[2][2])
        w = [jnp.exp2(p[2] - m) for p in parts]
        n = parts[0][0] * w[0] + parts[1][0] * w[1] + parts[2][0] * w[2]
        dn = parts[0][1] * w[0] + parts[1][1] * w[1] + parts[2][1] * w[2]
        o_ref[idx, :] = ((n / dn) * z_ref[idx, :].astype(F32)).astype(o_ref.dtype)

    pending = (group_batches(2, q2_ref, k2_ref, v2_ref, store_to(2, 0))
               + group_batches(1, q1_ref, k1_ref, v1_ref, store_to(1, 1))
               + group_batches(0, q0_ref, k0_ref, v0_ref, emit_last))
    active = []
    while pending or active:
        if pending:
            active.append(pending.pop(0))
        for gen in list(active):
            try:
                next(gen)
            except StopIteration:
                active.remove(gen)


def _attn_call(p2, off0, zb_off, g1, g2, B, S):
    H = DIL_HEADS
    p2v = p2.reshape(B, S, p2.shape[1])
    g1v = g1.reshape(B, 4, S // 4, GROUP_W)
    g2v = g2.reshape(B, 16, S // 16, GROUP_W)
    b0 = off0 // LANES
    zb0 = zb_off // LANES

    def nat(cb):
        return pl.BlockSpec((None, S, LANES), lambda b, h: (b, 0, cb + h))

    def perm(d, cb):
        return pl.BlockSpec((None, d, S // d, LANES), lambda b, h: (b, 0, 0, cb + h))

    return pl.pallas_call(
        _attn_kernel,
        grid=(B, H),
        in_specs=[nat(b0), nat(b0 + H), nat(b0 + 2 * H),
                  perm(4, 0), perm(4, H), perm(4, 2 * H),
                  perm(16, 0), perm(16, H), perm(16, 2 * H),
                  nat(zb0)],
        out_specs=pl.BlockSpec((None, S, LANES), lambda b, h: (b, 0, h)),
        out_shape=jax.ShapeDtypeStruct((B, S, DIL_W), BF16),
        scratch_shapes=[pltpu.VMEM((2, S, LANES), F32)] * 3,
        compiler_params=pltpu.CompilerParams(dimension_semantics=("arbitrary", "arbitrary"),
                                             vmem_limit_bytes=VMEM_LIMIT),
        name="dilated_attention",
    )(p2v, p2v, p2v, g1v, g1v, g1v, g2v, g2v, g2v, p2v)


def _out_kernel(oa_ref, ob_ref, ga_ref, gb_ref, x_ref, wa_ref, wb_ref, wo_ref, fw_ref, o_ref, *, final):
    ya = jnp.dot(oa_ref[...], wa_ref[...], preferred_element_type=F32)
    yb = jnp.dot(ob_ref[...], wb_ref[...], preferred_element_type=F32)
    merged = ga_ref[...].astype(F32) * ya + gb_ref[...].astype(F32) * yb
    xn = x_ref[...] + jnp.dot(merged.astype(BF16), wo_ref[...], preferred_element_type=F32)
    if final:
        xn = xn * lax.rsqrt(jnp.mean(xn * xn, axis=-1, keepdims=True) + NORM_EPS) * fw_ref[...]
    o_ref[...] = xn


def _out_call(oa, ob, p2, ga_off, gb_off, x2d, wa, wb, wo, fw, final):
    T, D = x2d.shape
    tm = OUT_TM
    ga_b = ga_off // D
    gb_b = gb_off // D
    const = lambda i: (0, 0)
    return pl.pallas_call(
        functools.partial(_out_kernel, final=final),
        grid=(T // tm,),
        in_specs=[pl.BlockSpec((tm, DN_W), lambda i: (i, 0)),
                  pl.BlockSpec((tm, DIL_W), lambda i: (i, 0)),
                  pl.BlockSpec((tm, D), lambda i: (i, ga_b)),
                  pl.BlockSpec((tm, D), lambda i: (i, gb_b)),
                  pl.BlockSpec((tm, D), lambda i: (i, 0)),
                  pl.BlockSpec((DN_W, D), const),
                  pl.BlockSpec((DIL_W, D), const),
                  pl.BlockSpec((D, D), const),
                  pl.BlockSpec((1, D), const)],
        out_specs=pl.BlockSpec((tm, D), lambda i: (i, 0)),
        out_shape=jax.ShapeDtypeStruct((T, D), F32),
        compiler_params=pltpu.CompilerParams(dimension_semantics=("arbitrary",),
                                             vmem_limit_bytes=VMEM_LIMIT),
        name="out_merge",
    )(oa, ob, p2, p2, x2d, wa, wb, wo, fw)


WPREP_TN = 512
WPREP_SRC_BLOCKS = WPREP_TN // LANES + 1


def _wprep_kernel(tbl_ref, *refs):
    src_refs, o_ref = refs[:WPREP_SRC_BLOCKS], refs[WPREP_SRC_BLOCKS]
    t = pl.program_id(0)
    shift = tbl_ref[1, t]
    kind = tbl_ref[2, t]
    x = jnp.concatenate([r[...] for r in src_refs], axis=0)
    row = lax.broadcasted_iota(jnp.int32, (WPREP_TN, 1), 0)
    factor = jnp.where(kind == 1, jnp.float32(DIL_DH ** -0.5 * LOG2E), jnp.float32(1.0))
    keep = jnp.where((kind != 2) | (row < 2 * DN_HEADS), factor, 0.0)
    for off in WPREP_SHIFTS:
        @pl.when(shift == off)
        def _(off=off):
            o_ref[...] = (x[off:off + WPREP_TN, :] * keep).T.astype(o_ref.dtype)


def _wprep_plan():
    starts, acc = {}, 0
    for name, size in zip(("q_a", "k_a", "v_a", "z_a", "b_a", "a_a", "q_b", "k_b", "v_b", "z_b", "g_a", "g_b"), PROJ_SIZES):
        starts[name] = acc
        acc += size

    def group(g):
        return [(starts[n] + g * DIL_W, 1 if n == "q_b" else 0) for n in ("q_b", "k_b", "v_b")]

    def span(name, width):
        return [(starts[name] + o, 0) for o in range(0, width, WPREP_TN)]

    plan = (group(1) + group(2) + span("q_a", DN_W) + span("k_a", DN_W) + span("v_a", DN_W) + span("z_a", DN_W)
            + span("g_a", D_MODEL) + span("g_b", D_MODEL) + span("z_b", DIL_W) + group(0) + [(starts["b_a"], 2)])
    return plan, acc


_WPREP_PLAN, _PROJ_W = _wprep_plan()
WPREP_SHIFTS = tuple(sorted({src % LANES for src, _ in _WPREP_PLAN}))


def _wprep_call(w_in, layer):
    D = w_in.shape[1]
    assert w_in.shape[2] == _PROJ_W
    tbl = jnp.asarray(np.array([[s // LANES for s, _ in _WPREP_PLAN], [s % LANES for s, _ in _WPREP_PLAN],
                                [k for _, k in _WPREP_PLAN]], dtype=np.int32))
    n_tiles = len(_WPREP_PLAN)
    last_blk = (_PROJ_W - 1) // LANES

    assert all(off % 8 == 0 for off in WPREP_SHIFTS)
    w_t = jnp.swapaxes(w_in, 1, 2)

    def src_spec(u):
        return pl.BlockSpec((None, LANES, D), lambda t, tbl: (layer, jnp.minimum(tbl[0, t] + u, last_blk), 0))

    return pl.pallas_call(
        _wprep_kernel,
        grid_spec=pltpu.PrefetchScalarGridSpec(
            num_scalar_prefetch=1, grid=(n_tiles,),
            in_specs=[src_spec(u) for u in range(WPREP_SRC_BLOCKS)],
            out_specs=pl.BlockSpec((D, WPREP_TN), lambda t, tbl: (0, t))),
        out_shape=jax.ShapeDtypeStruct((D, n_tiles * WPREP_TN), BF16),
        compiler_params=pltpu.CompilerParams(dimension_semantics=("arbitrary",), vmem_limit_bytes=VMEM_LIMIT),
        name="weight_relayout",
    )(tbl, *([w_t] * WPREP_SRC_BLOCKS))


def _layer(x, norm_w, w_in, layer, conv_w, a_log, dt_bias, dn_norm_w, w_o_dn, w_o_dil, w_out, final_w, final):
    B, S, D = x.shape
    T = B * S
    w_all = _wprep_call(w_in, layer)
    g1_col, g2_col, w1_col = 0, GROUP_W, 2 * GROUP_W
    w2_col = w1_col + 3 * DN_W
    small_col = w2_col + 5 * D
    ga_off = DN_W
    gb_off = ga_off + D
    zb_off = gb_off + D
    off0 = zb_off + DIL_W
    tn = D
    tiles2 = (((0, tn, "silu"),), ((0, tn, "sigmoid"),), ((0, tn, "sigmoid"),),
              ((0, DIL_W, "silu"), (DIL_W, tn, "none")), ((0, tn, "none"),))
    plain = (((0, GROUP_W, "none"),),)

    h, h4, h16, small = _norm_call(x, norm_w.astype(F32), w_all, small_col)
    qkv = _proj_conv_call(h.reshape(T, D), w_all, w1_col, 3 * DN_W, conv_w, S, "proj_conv_deltanet")
    p2 = _proj_call(h.reshape(T, D), w_all, w2_col, tn, tiles2, BF16, "proj_gates_group0")
    g1 = _proj_call(h4.reshape(T, D), w_all, g1_col, GROUP_W, plain, BF16, "proj_group1")
    g2 = _proj_call(h16.reshape(T, D), w_all, g2_col, GROUP_W, plain, BF16, "proj_group2")

    o_a = _deltanet_call(qkv, small.reshape(T, LANES), p2, a_log, dt_bias, dn_norm_w, B, S)
    o_b = _attn_call(p2, off0, zb_off, g1, g2, B, S)
    out = _out_call(o_a, o_b.reshape(T, DIL_W), p2, ga_off, gb_off, x.reshape(T, D),
                    w_o_dn.astype(BF16), w_o_dil.astype(BF16), w_out.astype(BF16),
                    final_w.astype(F32).reshape(1, D), final)
    return out.reshape(B, S, D)


def kernel(x, norm_w, w_in, conv_w, a_log, dt_bias, dn_norm_w, w_o_dn, w_o_dil, w_out, final_norm_w):
    depth = norm_w.shape[0]
    for l in range(depth):
        x = _layer(x, norm_w[l], w_in, l, conv_w[l], a_log[l], dt_bias[l], dn_norm_w[l],
                   w_o_dn[l], w_o_dil[l], w_out[l], final_norm_w, final=(l == depth - 1))
    return x
```

```python
import functools
import math

import jax
import jax.numpy as jnp
import numpy as np
from jax import lax
from jax.experimental import pallas as pl
from jax.experimental.pallas import tpu as pltpu

F32 = jnp.float32
BF16 = jnp.bfloat16

D_MODEL = 1024
DN_HEADS = 8
DN_DK = 128
DN_DV = 128
DN_CONV = 4
DN_CHUNK = 64
DIL_GROUPS = ((128, 1), (512, 4), (2048, 16))
DIL_HEADS = 4
DIL_DH = 128
ATT_BLOCK = 128
NORM_EPS = 1e-6
LOG2E = math.log2(math.e)
N_DIL = len(DIL_GROUPS)
DN_W = DN_HEADS * DN_DK
DIL_W = DIL_HEADS * DIL_DH
GROUP_W = 3 * DIL_W
PROJ_SIZES = (DN_W, DN_W, DN_W, DN_W, DN_HEADS, DN_HEADS,
              N_DIL * DIL_W, N_DIL * DIL_W, N_DIL * DIL_W, DIL_W, D_MODEL, D_MODEL)

LANES = 128
VMEM_LIMIT = 56 * 1024 * 1024

NORM_TM = 1024
PROJ_TM = 4096
CONV_TM = 2048
PROJ_TN = 1024
PROJ_SUB = 512
CONV_SUB = 512
DN_TB = 512
DN_UNIT = 2
ATT_BATCH = 8
OUT_TM = 1024


def _sigmoid(x):
    return 1.0 / (1.0 + jnp.exp2(x * (-LOG2E)))


def _silu(x):
    return x * _sigmoid(x)


def _softplus(x):
    return jnp.maximum(x, 0.0) + jnp.log1p(jnp.exp(-jnp.abs(x)))


def _mm(a, b):
    return lax.dot_general(a.astype(BF16), b.astype(BF16), (((1,), (0,)), ((), ())),
                           preferred_element_type=F32)


def _mm_nt(a, b):
    return lax.dot_general(a.astype(BF16), b.astype(BF16), (((1,), (1,)), ((), ())),
                           preferred_element_type=F32)


def _mm_tn(a, b):
    return lax.dot_general(a.astype(BF16), b.astype(BF16), (((0,), (0,)), ((), ())),
                           preferred_element_type=F32)


def _norm_kernel(x_ref, w_ref, ws_ref, h_ref, h4_ref, h16_ref, sm_ref, slab_ref):
    x = x_ref[...]
    y = x * lax.rsqrt(jnp.mean(x * x, axis=-1, keepdims=True) + NORM_EPS) * w_ref[...]
    hb = y.astype(BF16)
    h_ref[...] = hb
    sm_ref[...] = jnp.dot(hb, ws_ref[...], preferred_element_type=F32)
    n_slabs = D_MODEL // LANES
    for c in range(n_slabs):
        slab_ref[c] = y[:, c * LANES:(c + 1) * LANES]
    tm = x.shape[0]
    for d, out in ((4, h4_ref), (16, h16_ref)):
        n = tm // d
        for r in range(d):
            for c in range(n_slabs):
                out[r, :, c * LANES:(c + 1) * LANES] = slab_ref[c, pl.ds(r, n, stride=d), :].astype(BF16)


def _norm_call(x, w, w_all, small_col):
    B, S, D = x.shape
    tm = NORM_TM
    nt = S // tm
    small_blk = small_col // LANES
    return pl.pallas_call(
        _norm_kernel,
        grid=(B, nt),
        in_specs=[pl.BlockSpec((None, tm, D), lambda b, i: (b, i, 0)),
                  pl.BlockSpec((1, D), lambda b, i: (0, 0)),
                  pl.BlockSpec((D, LANES), lambda b, i: (0, small_blk))],
        out_specs=[pl.BlockSpec((None, tm, D), lambda b, i: (b, i, 0)),
                   pl.BlockSpec((None, 4, tm // 4, D), lambda b, i: (b, 0, i, 0)),
                   pl.BlockSpec((None, 16, tm // 16, D), lambda b, i: (b, 0, i, 0)),
                   pl.BlockSpec((None, tm, LANES), lambda b, i: (b, i, 0))],
        out_shape=[jax.ShapeDtypeStruct((B, S, D), BF16),
                   jax.ShapeDtypeStruct((B, 4, S // 4, D), BF16),
                   jax.ShapeDtypeStruct((B, 16, S // 16, D), BF16),
                   jax.ShapeDtypeStruct((B, S, LANES), F32)],
        scratch_shapes=[pltpu.VMEM((D // LANES, tm, LANES), F32)],
        compiler_params=pltpu.CompilerParams(dimension_semantics=("arbitrary", "arbitrary"),
                                             vmem_limit_bytes=VMEM_LIMIT),
        name="norm_deinterleave",
    )(x, w.reshape(1, D), w_all)


_ACTS = {"none": lambda v: v, "silu": _silu, "sigmoid": _sigmoid}


def _proj_kernel(h_ref, w_ref, o_ref, *, tiles):
    tm = h_ref.shape[0]
    sub = PROJ_SUB

    def run(parts):
        for m in range(tm // sub):
            rows = slice(m * sub, (m + 1) * sub)
            acc = jnp.dot(h_ref[rows, :], w_ref[...], preferred_element_type=F32)
            for lo, hi, act in parts:
                o_ref[rows, lo:hi] = _ACTS[act](acc[:, lo:hi]).astype(o_ref.dtype)

    kinds = sorted(set(tiles), key=tiles.index)
    if len(kinds) == 1:
        run(kinds[0])
        return
    j = pl.program_id(1)
    for parts in kinds:
        cond = functools.reduce(lambda a, b: a | b, [j == t for t, p in enumerate(tiles) if p == parts])
        pl.when(cond)(functools.partial(run, parts))


def _proj_call(h2d, w_all, col0, tn, tiles, out_dtype, name):
    T, D = h2d.shape
    N = len(tiles) * tn
    tm = PROJ_TM
    assert col0 % tn == 0 and T % tm == 0
    blk0 = col0 // tn
    return pl.pallas_call(
        functools.partial(_proj_kernel, tiles=tiles),
        grid=(T // tm, N // tn),
        in_specs=[pl.BlockSpec((tm, D), lambda i, j: (i, 0)),
                  pl.BlockSpec((D, tn), lambda i, j: (0, blk0 + j))],
        out_specs=pl.BlockSpec((tm, tn), lambda i, j: (i, j)),
        out_shape=jax.ShapeDtypeStruct((T, N), out_dtype),
        compiler_params=pltpu.CompilerParams(dimension_semantics=("arbitrary", "arbitrary"),
                                             vmem_limit_bytes=VMEM_LIMIT),
        name=name,
    )(h2d, w_all)


def _proj_conv_kernel(h_ref, w_ref, cw_ref, o_ref, u_ref, y_ref, *, tiles_per_seq):
    i = pl.program_id(1)
    tm = h_ref.shape[0]
    sub = CONV_SUB
    half = sub // 2
    n_slab = u_ref.shape[0]

    @pl.when(i % tiles_per_seq == 0)
    def _():
        u_ref[:, 0:8, :] = jnp.zeros((n_slab, 8, LANES), F32)

    @pl.when(i % tiles_per_seq != 0)
    def _():
        u_ref[:, 0:8, :] = u_ref[:, tm:tm + 8, :]

    for m in range(tm // sub):
        base = 8 + m * sub
        acc = jnp.dot(h_ref[m * sub:(m + 1) * sub, :], w_ref[...], preferred_element_type=F32)
        for c in range(n_slab):
            u_ref[c, base:base + sub, :] = acc[:, c * LANES:(c + 1) * LANES]
        for c in range(n_slab):
            taps = [cw_ref[k:k + 1, c * LANES:(c + 1) * LANES] for k in range(DN_CONV)]
            u = [u_ref[c, pl.ds(base - (DN_CONV - 1) + k, half, stride=2), :] for k in range(DN_CONV + 1)]
            for phase in range(2):
                y = taps[0] * u[phase]
                for k in range(1, DN_CONV):
                    y = y + taps[k] * u[phase + k]
                y_ref[c, pl.ds(phase, half, stride=2), :] = y + y * jnp.tanh(y)
            o_ref[m * sub:(m + 1) * sub, c * LANES:(c + 1) * LANES] = y_ref[c].astype(o_ref.dtype)


def _proj_conv_call(h2d, w_all, col0, N, conv_w, S, name):
    T, D = h2d.shape
    tm, tn = CONV_TM, PROJ_TN
    assert N % tn == 0 and col0 % tn == 0 and T % tm == 0 and S % tm == 0
    blk0 = col0 // tn
    cw = jnp.zeros((8, N), F32).at[0:DN_CONV].set(0.5 * conv_w.astype(F32))
    return pl.pallas_call(
        functools.partial(_proj_conv_kernel, tiles_per_seq=S // tm),
        grid=(N // tn, T // tm),
        in_specs=[pl.BlockSpec((tm, D), lambda j, i: (i, 0)),
                  pl.BlockSpec((D, tn), lambda j, i: (0, blk0 + j)),
                  pl.BlockSpec((8, tn), lambda j, i: (0, j))],
        out_specs=pl.BlockSpec((tm, tn), lambda j, i: (i, j)),
        out_shape=jax.ShapeDtypeStruct((T, N), BF16),
        scratch_shapes=[pltpu.VMEM((tn // LANES, tm + 8, LANES), F32),
                        pltpu.VMEM((tn // LANES, CONV_SUB, LANES), F32)],
        compiler_params=pltpu.CompilerParams(dimension_semantics=("arbitrary", "arbitrary"),
                                             vmem_limit_bytes=VMEM_LIMIT),
        name=name,
    )(h2d, w_all, cw)


def _deltanet_kernel(qkv_ref, sm_ref, z_ref, hp_ref, nw_ref, o_ref, st_ref):
    C = DN_CHUNK
    tb = o_ref.shape[0]

    @pl.when(pl.program_id(1) == 0)
    def _():
        st_ref[...] = jnp.zeros_like(st_ref)

    row = lax.broadcasted_iota(jnp.int32, (C, C), 0)
    col = lax.broadcasted_iota(jnp.int32, (C, C), 1)
    causal = row >= col
    strict = row > col
    eye =(row == col).astype(F32)
    hp = hp_ref[...]
    neg_a = -jnp.exp(hp[0:1, :])
    dt_b = hp[1:2, :]
    nw = nw_ref[...]

    heads = range(DN_HEADS)

    def bcast(a, lane):
        return jnp.broadcast_to(a[:, lane:lane + 1], (C, LANES))

    def load(base, c0):
        return qkv_ref[base:base + C, c0:c0 + LANES].astype(F32)

    row_l = lax.broadcasted_iota(jnp.int32, (C, LANES), 0)

    def cumsum_rows(v):
        shift = 1
        while shift < C:
            v = v + jnp.where(row_l >= shift, pltpu.roll(v, shift, axis=0), 0.0)
            shift *= 2
        return v

    def prologue(unit, out):
        dec = {}
        for c in unit:
            sm = sm_ref[c * C:(c + 1) * C, :]
            g_all = neg_a * _softplus(sm + dt_b)
            gc_all = cumsum_rows(g_all)
            gc_t = jnp.concatenate([gc_all, jnp.zeros_like(gc_all)], axis=0).T
            glast_all = gc_all[C - 1:C, :]
            dec[c] = dict(beta=_sigmoid(sm), gc=gc_all, gc_t=gc_t, eg=jnp.exp(gc_all),
                          ek=jnp.exp(glast_all - gc_all), dl=jnp.exp(glast_all))
            yield
        q, k, kb, eg, gamma, rhs = [], [], [], [], [], []
        for c in unit:
            for h in heads:
                lane = DN_HEADS + h
                base = c * C
                qh = load(base, h * DN_DK)
                kh = load(base, DN_W + h * DN_DK)
                vh = load(base, 2 * DN_W + h * DN_DV)
                qh = qh * (lax.rsqrt(jnp.sum(qh * qh, axis=-1, keepdims=True) + NORM_EPS) * (DN_DK ** -0.5))
                kh = kh * lax.rsqrt(jnp.sum(kh * kh, axis=-1, keepdims=True) + NORM_EPS)
                beta = bcast(dec[c]["beta"], h)
                egh = bcast(dec[c]["eg"], lane)
                kbh = kh * beta
                q.append(qh)
                k.append(kh)
                kb.append(kbh)
                eg.append(egh)
                rhs.append(jnp.concatenate([vh * beta, kbh * egh], axis=1))
                diff = bcast(dec[c]["gc"], lane)[:, 0:C] - dec[c]["gc_t"][lane:lane + 1, 0:C]
                gamma.append(jnp.exp(jnp.where(causal, diff, -jnp.inf)))
                if h % 2 == 1:
                    yield
        out.update(dec=dec, q=q, k=k, kb=kb, eg=eg, gamma=gamma, rhs=rhs)

    def solve(unit, pro, out):
        idx = range(len(unit) * DN_HEADS)
        q, k, kb, gamma, rhs = pro["q"], pro["k"], pro["kb"], pro["gamma"], pro["rhs"]
        a = [_mm_nt(jnp.concatenate([kb[i], q[i]], axis=0), k[i]) for i in idx]
        yield
        p = [jnp.where(strict, -(a[i][0:C] * gamma[i]), 0.0) for i in idx]
        aqk = [a[i][C:] * gamma[i] for i in idx]
        t = [eye + p[i] for i in idx]
        p = [_mm(p[i], p[i]) for i in idx]
        yield
        for _ in range(4):
            r = [_mm(jnp.concatenate([p[i], t[i]], axis=0), p[i]) for i in idx]
            t = [t[i] + r[i][C:] for i in idx]
            p = [r[i][0:C] for i in idx]
            yield
        t = [t[i] + _mm(t[i], p[i]) for i in idx]
        yield
        out.update(x=[_mm(t[i], rhs[i]) for i in idx], aqk=aqk)

    state = [st_ref[h] for h in heads]

    def recur(unit, pro, sol):
        for n, c in enumerate(unit):
            base = c * C
            at = lambda lst, h: lst[n * DN_HEADS + h]
            ws_qs = [_mm(jnp.concatenate([at(sol["x"], h)[:, DN_DV:], at(pro["q"], h) * at(pro["eg"], h)], axis=0),
                         state[h]) for h in heads]
            yield
            v_new = [at(sol["x"], h)[:, 0:DN_DV] - ws_qs[h][0:C] for h in heads]
            o = [ws_qs[h][C:] + _mm(at(sol["aqk"], h), v_new[h]) for h in heads]
            for h in heads:
                lane = DN_HEADS + h
                kd = at(pro["k"], h) * bcast(pro["dec"][c]["ek"], lane)
                dl = jnp.broadcast_to(pro["dec"][c]["dl"][:, lane:lane + 1], (DN_DK, DN_DV))
                state[h] = state[h] * dl + _mm_tn(kd, v_new[h])
            yield
            for h in heads:
                y = o[h] * lax.rsqrt(jnp.mean(o[h] * o[h], axis=-1, keepdims=True) + NORM_EPS) * nw
                gate = z_ref[base:base + C, h * DN_DV:(h + 1) * DN_DV].astype(F32)
                o_ref[base:base + C, h * DN_DV:(h + 1) * DN_DV] = (y * gate).astype(o_ref.dtype)
                if h % 4 == 3:
                    yield

    def interleave(gens):
        gens = list(gens)
        while gens:
            for g in list(gens):
                try:
                    next(g)
                except StopIteration:
                    gens.remove(g)

    n_chunks = tb // C
    units = [list(range(u, min(u + DN_UNIT, n_chunks))) for u in range(0, n_chunks, DN_UNIT)]
    pro = [dict() for _ in units]
    sol = [dict() for _ in units]
    for tick in range(len(units) + 2):
        active = []
        if 0 <= tick - 2 < len(units):
            active.append(recur(units[tick - 2], pro[tick - 2], sol[tick - 2]))
        if 0 <= tick - 1 < len(units):
            active.append(solve(units[tick - 1], pro[tick - 1], sol[tick - 1]))
        if tick < len(units):
            active.append(prologue(units[tick], pro[tick]))
        interleave(active)
    for h in heads:
        st_ref[h] = state[h]


def _deltanet_call(qkv, small, p2, a_log, dt_bias, dn_norm_w, B, S):
    tb = DN_TB
    nt = S // tb
    hp = jnp.zeros((8, LANES), F32)
    hp = hp.at[0, DN_HEADS:2 * DN_HEADS].set(a_log.astype(F32)).at[1, DN_HEADS:2 * DN_HEADS].set(dt_bias.astype(F32))
    nw = dn_norm_w.astype(F32).reshape(1, DN_DV)
    return pl.pallas_call(
        _deltanet_kernel,
        grid=(B, nt),
        in_specs=[pl.BlockSpec((tb, 3 * DN_W), lambda b, i: (b * nt + i, 0)),
                  pl.BlockSpec((tb, LANES), lambda b, i: (b * nt + i, 0)),
                  pl.BlockSpec((tb, DN_W), lambda b, i: (b * nt + i, 0)),
                  pl.BlockSpec((8, LANES), lambda b, i: (0, 0)),
                  pl.BlockSpec((1, DN_DV), lambda b, i: (0, 0))],
        out_specs=pl.BlockSpec((tb, DN_W), lambda b, i: (b * nt + i, 0)),
        out_shape=jax.ShapeDtypeStruct((B * S, DN_W), BF16),
        scratch_shapes=[pltpu.VMEM((DN_HEADS, DN_DK, DN_DV), F32)],
        compiler_params=pltpu.CompilerParams(dimension_semantics=("arbitrary", "arbitrary"),
                                             vmem_limit_bytes=VMEM_LIMIT),
        name="deltanet",
    )(qkv, small, p2, hp, nw)


def _alibi_slope(idx):
    n = N_DIL * DIL_HEADS
    return 2.0 ** (-8.0 * (idx + 1) / n)


def _attn_kernel(q0_ref, k0_ref, v0_ref, q1_ref, k1_ref, v1_ref, q2_ref, k2_ref, v2_ref, z_ref, o_ref,
                 num_ref, den_ref, mx_ref):
    Q = ATT_BLOCK
    hd = pl.program_id(1)
    span = DIL_GROUPS[0][0] // DIL_GROUPS[0][1]
    dist = (Q + lax.broadcasted_iota(jnp.int32, (Q, 2 * Q), 0) - lax.broadcasted_iota(jnp.int32, (Q, 2 * Q), 1))
    valid = (dist >= 0) & (dist <= span)
    distf = dist.astype(F32)

    def slope_of(g):
        s = jnp.float32(_alibi_slope(g * DIL_HEADS))
        for h in range(1, DIL_HEADS):
            s = jnp.where(hd == h, jnp.float32(_alibi_slope(g * DIL_HEADS + h)), s)
        return s

    def group_batches(g, q_ref, k_ref, v_ref, emit):
        d = DIL_GROUPS[g][1]
        bias2 = jnp.where(valid, -(slope_of(g) * (float(d) * LOG2E)) * distf, -jnp.inf)
        bias1 = bias2[:, Q:]
        nb = q_ref.shape[-2] // Q

        def item(r, j):
            qr, kr, vr = (q_ref, k_ref, v_ref) if d == 1 else (q_ref.at[r], k_ref.at[r], v_ref.at[r])
            if j == 0:
                return qr[0:Q, :], kr[0:Q, :], vr[0:Q, :], bias1
            lo, cur = (j - 1) * Q, j * Q
            return qr[cur:cur + Q, :], kr[lo:lo + 2 * Q, :], vr[lo:lo + 2 * Q, :], bias2

        def batch(ids):
            items = [item(r, j) for r, j in ids]
            s = [_mm_nt(q, k) + bias for q, k, v, bias in items]
            mx = [jnp.max(si, axis=-1, keepdims=True) for si in s]
            yield
            p = [jnp.exp2(si - mi) for si, mi in zip(s, mx)]
            den = [jnp.sum(pi, axis=-1, keepdims=True) for pi in p]
            num = [_mm(pi, it[2]) for pi, it in zip(p, items)]
            yield
            for (r, j), n, dn, m in zip(ids, num, den, mx):
                emit(r, j, (n, jnp.broadcast_to(dn, (Q, LANES)), jnp.broadcast_to(m, (Q, LANES))))

        ids = [(r, j) for r in range(d) for j in range(nb)]
        return [batch(ids[i:i + ATT_BATCH]) for i in range(0, len(ids), ATT_BATCH)]

    def rows(g, r, j):
        d = DIL_GROUPS[g][1]
        if d == 1:
            return pl.ds(j * Q, Q)
        return pl.ds(j * (Q * d) + r, Q, stride=d)

    def store_to(g, slot):
        def emit(r, j, res):
            idx = rows(g, r, j)
            num_ref[slot, idx, :] = res[0]
            den_ref[slot, idx, :] = res[1]
            mx_ref[slot, idx, :] = res[2]
        return emit

    def emit_last(r, j, res):
        idx = rows(0, r, j)
        parts = [(num_ref[s, idx, :], den_ref[s, idx, :], mx_ref[s, idx, :]) for s in range(2)] + [res]
        m = jnp.maximum(jnp.maximum(parts[0][2], parts[1][2]), parts[2][2])
        w = [jnp.exp2(p[2] - m) for p in parts]
        n = parts[0][0] * w[0] + parts[1][0] * w[1] + parts[2][0] * w[2]
        dn = parts[0][1] * w[0] + parts[1][1] * w[1] + parts[2][1] * w[2]
        o_ref[idx, :] = ((n / dn) * z_ref[idx, :].astype(F32)).astype(o_ref.dtype)

    pending = (group_batches(2, q2_ref, k2_ref, v2_ref, store_to(2, 0))
               + group_batches(1, q1_ref, k1_ref, v1_ref, store_to(1, 1))
               + group_batches(0, q0_ref, k0_ref, v0_ref, emit_last))
    active = []
    while pending or active:
        if pending:
            active.append(pending.pop(0))
        for gen in list(active):
            try:
                next(gen)
            except StopIteration:
                active.remove(gen)


def _attn_call(p2, off0, zb_off, g1, g2, B, S):
    H = DIL_HEADS
    p2v = p2.reshape(B, S, p2.shape[1])
    g1v = g1.reshape(B, 4, S // 4, GROUP_W)
    g2v = g2.reshape(B, 16, S // 16, GROUP_W)
    b0 = off0 // LANES
    zb0 = zb_off // LANES

    def nat(cb):
        return pl.BlockSpec((None, S, LANES), lambda b, h: (b, 0, cb + h))

    def perm(d, cb):
        return pl.BlockSpec((None, d, S // d, LANES), lambda b, h: (b, 0, 0, cb + h))

    return pl.pallas_call(
        _attn_kernel,
        grid=(B, H),
        in_specs=[nat(b0), nat(b0 + H), nat(b0 + 2 * H),
                  perm(4, 0), perm(4, H), perm(4, 2 * H),
                  perm(16, 0), perm(16, H), perm(16, 2 * H),
                  nat(zb0)],
        out_specs=pl.BlockSpec((None, S, LANES), lambda b, h: (b, 0, h)),
        out_shape=jax.ShapeDtypeStruct((B, S, DIL_W), BF16),
        scratch_shapes=[pltpu.VMEM((2, S, LANES), F32)] * 3,
        compiler_params=pltpu.CompilerParams(dimension_semantics=("arbitrary", "arbitrary"),
                                             vmem_limit_bytes=VMEM_LIMIT),
        name="dilated_attention",
    )(p2v, p2v, p2v, g1v, g1v, g1v, g2v, g2v, g2v, p2v)


def _out_kernel(oa_ref, ob_ref, ga_ref, gb_ref, x_ref, wa_ref, wb_ref, wo_ref, fw_ref, o_ref, *, final):
    ya = jnp.dot(oa_ref[...], wa_ref[...], preferred_element_type=F32)
    yb = jnp.dot(ob_ref[...], wb_ref[...], preferred_element_type=F32)
    merged = ga_ref[...].astype(F32) * ya + gb_ref[...].astype(F32) * yb
    xn = x_ref[...] + jnp.dot(merged.astype(BF16), wo_ref[...], preferred_element_type=F32)
    if final:
        xn = xn * lax.rsqrt(jnp.mean(xn * xn, axis=-1, keepdims=True) + NORM_EPS) * fw_ref[...]
    o_ref[...] = xn


def _out_call(oa, ob, p2, ga_off, gb_off, x2d, wa, wb, wo, fw, final):
    T, D = x2d.shape
    tm = OUT_TM
    ga_b = ga_off // D
    gb_b = gb_off // D
    const = lambda i: (0, 0)
    return pl.pallas_call(
        functools.partial(_out_kernel, final=final),
        grid=(T // tm,),
        in_specs=[pl.BlockSpec((tm, DN_W), lambda i: (i, 0)),
                  pl.BlockSpec((tm, DIL_W), lambda i: (i, 0)),
                  pl.BlockSpec((tm, D), lambda i: (i, ga_b)),
                  pl.BlockSpec((tm, D), lambda i: (i, gb_b)),
                  pl.BlockSpec((tm, D), lambda i: (i, 0)),
                  pl.BlockSpec((DN_W, D), const),
                  pl.BlockSpec((DIL_W, D), const),
                  pl.BlockSpec((D, D), const),
                  pl.BlockSpec((1, D), const)],
        out_specs=pl.BlockSpec((tm, D), lambda i: (i, 0)),
        out_shape=jax.ShapeDtypeStruct((T, D), F32),
        compiler_params=pltpu.CompilerParams(dimension_semantics=("arbitrary",),
                                             vmem_limit_bytes=VMEM_LIMIT),
        name="out_merge",
    )(oa, ob, p2, p2, x2d, wa, wb, wo, fw)


WPREP_TN = 512
WPREP_SRC_BLOCKS = WPREP_TN // LANES + 1


def _wprep_kernel(tbl_ref, *refs):
    src_refs, o_ref = refs[:WPREP_SRC_BLOCKS], refs[WPREP_SRC_BLOCKS]
    t = pl.program_id(0)
    shift = tbl_ref[1, t]
    kind = tbl_ref[2, t]
    x = jnp.concatenate([r[...] for r in src_refs], axis=0)
    row = lax.broadcasted_iota(jnp.int32, (WPREP_TN, 1), 0)
    factor = jnp.where(kind == 1, jnp.float32(DIL_DH ** -0.5 * LOG2E), jnp.float32(1.0))
    keep = jnp.where((kind != 2) | (row < 2 * DN_HEADS), factor, 0.0)
    for off in WPREP_SHIFTS:
        @pl.when(shift == off)
        def _(off=off):
            o_ref[...] = (x[off:off + WPREP_TN, :] * keep).T.astype(o_ref.dtype)


def _wprep_plan():
    starts, acc = {}, 0
    for name, size in zip(("q_a", "k_a", "v_a", "z_a", "b_a", "a_a", "q_b", "k_b", "v_b", "z_b", "g_a", "g_b"), PROJ_SIZES):
        starts[name] = acc
        acc += size

    def group(g):
        return [(starts[n] + g * DIL_W, 1 if n == "q_b" else 0) for n in ("q_b", "k_b", "v_b")]

    def span(name, width):
        return [(starts[name] + o, 0) for o in range(0, width, WPREP_TN)]

    plan = (group(1) + group(2) + span("q_a", DN_W) + span("k_a", DN_W) + span("v_a", DN_W) + span("z_a", DN_W)
            + span("g_a", D_MODEL) + span("g_b", D_MODEL) + span("z_b", DIL_W) + group(0) + [(starts["b_a"], 2)])
    return plan, acc


_WPREP_PLAN, _PROJ_W = _wprep_plan()
WPREP_SHIFTS = tuple(sorted({src % LANES for src, _ in _WPREP_PLAN}))


def _wprep_call(w_in, layer):
    D = w_in.shape[1]
    assert w_in.shape[2] == _PROJ_W
    tbl = jnp.asarray(np.array([[s // LANES for s, _ in _WPREP_PLAN], [s % LANES for s, _ in _WPREP_PLAN],
                                [k for _, k in _WPREP_PLAN]], dtype=np.int32))
    n_tiles = len(_WPREP_PLAN)
    last_blk = (_PROJ_W - 1) // LANES

    assert all(off % 8 == 0 for off in WPREP_SHIFTS)
    w_t = jnp.swapaxes(w_in, 1, 2)

    def src_spec(u):
        return pl.BlockSpec((None, LANES, D), lambda t, tbl: (layer, jnp.minimum(tbl[0, t] + u, last_blk), 0))

    return pl.pallas_call(
        _wprep_kernel,
        grid_spec=pltpu.PrefetchScalarGridSpec(
            num_scalar_prefetch=1, grid=(n_tiles,),
            in_specs=[src_spec(u) for u in range(WPREP_SRC_BLOCKS)],
            out_specs=pl.BlockSpec((D, WPREP_TN), lambda t, tbl: (0, t))),
        out_shape=jax.ShapeDtypeStruct((D, n_tiles * WPREP_TN), BF16),
        compiler_params=pltpu.CompilerParams(dimension_semantics=("arbitrary",), vmem_limit_bytes=VMEM_LIMIT),
        name="weight_relayout",
    )(tbl, *([w_t] * WPREP_SRC_BLOCKS))


def _layer(x, norm_w, w_in, layer, conv_w, a_log, dt_bias, dn_norm_w, w_o_dn, w_o_dil, w_out, final_w, final):
    B, S, D = x.shape
    T = B * S
    w_all = _wprep_call(w_in, layer)
    g1_col, g2_col, w1_col = 0, GROUP_W, 2 * GROUP_W
    w2_col = w1_col + 3 * DN_W
    small_col = w2_col + 5 * D
    ga_off = DN_W
    gb_off = ga_off + D
    zb_off = gb_off + D
    off0 = zb_off + DIL_W
    tn = D
    tiles2 = (((0, tn, "silu"),), ((0, tn, "sigmoid"),), ((0, tn, "sigmoid"),),
              ((0, DIL_W, "silu"), (DIL_W, tn, "none")), ((0, tn, "none"),))
    plain = (((0, GROUP_W, "none"),),)

    h, h4, h16, small = _norm_call(x, norm_w.astype(F32), w_all, small_col)
    qkv = _proj_conv_call(h.reshape(T, D), w_all, w1_col, 3 * DN_W, conv_w, S, "proj_conv_deltanet")
    p2 = _proj_call(h.reshape(T, D), w_all, w2_col, tn, tiles2, BF16, "proj_gates_group0")
    g1 = _proj_call(h4.reshape(T, D), w_all, g1_col, GROUP_W, plain, BF16, "proj_group1")
    g2 = _proj_call(h16.reshape(T, D), w_all, g2_col, GROUP_W, plain, BF16, "proj_group2")

    o_a = _deltanet_call(qkv, small.reshape(T, LANES), p2, a_log, dt_bias, dn_norm_w, B, S)
    o_b = _attn_call(p2, off0, zb_off, g1, g2, B, S)
    out = _out_call(o_a, o_b.reshape(T, DIL_W), p2, ga_off, gb_off, x.reshape(T, D),
                    w_o_dn.astype(BF16), w_o_dil.astype(BF16), w_out.astype(BF16),
                    final_w.astype(F32).reshape(1, D), final)
    return out.reshape(B, S, D)


def kernel(x, norm_w, w_in, conv_w, a_log, dt_bias, dn_norm_w, w_o_dn, w_o_dil, w_out, final_norm_w):
    depth = norm_w.shape[0]
    for l in range(depth):
        x = _layer(x, norm_w[l], w_in, l, conv_w[l], a_log[l], dt_bias[l], dn_norm_w[l],
                   w_o_dn[l], w_o_dil[l], w_out[l], final_norm_w, final=(l == depth - 1))
    return x
```
